```python
import math
import jax, jax.numpy as jnp
from jax import lax
import numpy as np

D_MODEL = 4096
BATCH = 4
SEQ = 2048
DEPTH = 2
DEC_BATCH = 128
DEC_SEQ = 4
PAST_LEN = 16384
PAGE_SIZE = 128

HEAD_DIM = 128
N_HEADS = D_MODEL // HEAD_DIM
N_KV_HEADS = 8
GROUP = N_HEADS // N_KV_HEADS
WINDOW = 128
SWA_BLOCK = WINDOW
MLA_HEADS = 32
Q_LORA = 1024
KV_LORA = 512
QK_NOPE = 128
QK_ROPE = 64
V_HEAD = 128
MLA_SCALE = (QK_NOPE + QK_ROPE) ** -0.5
ATTN_BLOCK = 128
D_FF = 14336
N_EXPERTS = 8
TOP_K = 2
ROPE_THETA = 10000.0
NORM_EPS = 1e-6

kernel_name = 'hybrid_swa_mla_moe_step'


def rms_norm(x, g):
    xf = x.astype(jnp.float32)
    y = xf * lax.rsqrt(jnp.mean(xf * xf, axis=-1, keepdims=True) + NORM_EPS)
    return (y * g.astype(jnp.float32)).astype(x.dtype)


def rope_cos_sin(pos, dim):
    inv = jnp.power(jnp.float32(ROPE_THETA), -jnp.arange(0, dim, 2, dtype=jnp.float32) / dim)
    ang = pos.astype(jnp.float32)[:, None] * inv[None, :]
    return jnp.cos(ang), jnp.sin(ang)


def apply_rope(x, cos, sin):
    xf = x.astype(jnp.float32)
    x1, x2 = jnp.split(xf, 2, axis=-1)
    return jnp.concatenate([x1 * cos - x2 * sin, x2 * cos + x1 * sin], axis=-1).astype(x.dtype)


def sink_softmax(scores, mask, sink):
    s = jnp.where(mask, scores, -jnp.inf)
    m = jnp.maximum(jnp.max(s, axis=-1, keepdims=True), sink)
    e = jnp.exp(s - m)
    return e / (jnp.sum(e, axis=-1, keepdims=True) + jnp.exp(sink - m))


def swa_qkv(h, w_qkv, pos):
    n, t, _ = h.shape
    qkv = h @ w_qkv
    q = qkv[..., :N_HEADS * HEAD_DIM].reshape(n, t, N_HEADS, HEAD_DIM)
    k = qkv[..., N_HEADS * HEAD_DIM:(N_HEADS + N_KV_HEADS) * HEAD_DIM].reshape(n, t, N_KV_HEADS, HEAD_DIM)
    v = qkv[..., (N_HEADS + N_KV_HEADS) * HEAD_DIM:].reshape(n, t, N_KV_HEADS, HEAD_DIM)
    cos, sin = rope_cos_sin(pos, HEAD_DIM)
    q = apply_rope(q, cos[:, None], sin[:, None])
    k = apply_rope(k, cos[:, None], sin[:, None])
    return q, k, v


def swa_prompt(q, k, v, sinks):
    b, s = q.shape[:2]
    nb = s // SWA_BLOCK
    qb = q.reshape(b, nb, SWA_BLOCK, N_KV_HEADS, GROUP, HEAD_DIM)
    kb = k.reshape(b, nb, SWA_BLOCK, N_KV_HEADS, HEAD_DIM)
    vb = v.reshape(b, nb, SWA_BLOCK, N_KV_HEADS, HEAD_DIM)
    pad = jnp.zeros_like(kb[:, :1])
    kk = jnp.concatenate([jnp.concatenate([pad, kb[:, :-1]], axis=1), kb], axis=2)
    vv = jnp.concatenate([jnp.concatenate([pad, vb[:, :-1]], axis=1), vb], axis=2)
    scores = jnp.einsum('bnqkgd,bnskd->bnkgqs', qb, kk,
                        preferred_element_type=jnp.float32) * (HEAD_DIM ** -0.5)
    blk = jnp.arange(nb)[:, None, None]
    tq = blk * SWA_BLOCK + jnp.arange(SWA_BLOCK)[None, :, None]
    ts = (blk - 1) * SWA_BLOCK + jnp.arange(2 * SWA_BLOCK)[None, None, :]
    mask = (ts >= 0) & (tq - ts >= 0) & (tq - ts <= WINDOW)
    sink = sinks.astype(jnp.float32).reshape(N_KV_HEADS, GROUP)[:, :, None, None]
    p = sink_softmax(scores, mask[None, :, None, None], sink).astype(v.dtype)
    o = jnp.einsum('bnkgqs,bnskd->bnqkgd', p, vv)
    return o.reshape(b, s, N_HEADS * HEAD_DIM)


def swa_sample(q, k, v, buf_k, buf_v, sinks, past_len):
    n, t = q.shape[:2]
    w = buf_k.shape[1]
    kk = jnp.concatenate([buf_k, k], axis=1)
    vv = jnp.concatenate([buf_v, v], axis=1)
    qg = q.reshape(n, t, N_KV_HEADS, GROUP, HEAD_DIM)
    scores = jnp.einsum('btkgd,bskd->bkgts', qg, kk,
                        preferred_element_type=jnp.float32) * (HEAD_DIM ** -0.5)
    tq = past_len + jnp.arange(t)[:, None]
    ts = past_len - w + jnp.arange(w + t)[None, :]
    mask = (tq - ts >= 0) & (tq - ts <= WINDOW)
    sink = sinks.astype(jnp.float32).reshape(N_KV_HEADS, GROUP)[:, :, None, None]
    p = sink_softmax(scores, mask, sink).astype(v.dtype)
    o = jnp.einsum('bkgts,bskd->btkgd', p, vv).reshape(n, t, N_HEADS * HEAD_DIM)
    return o, kk[:, -w:], vv[:, -w:]


def mla_project(h, w_dq, g_q, w_uq, w_dkv, g_kv, w_uk, pos):
    n, t, _ = h.shape
    cq = rms_norm(h @ w_dq, g_q)
    q = (cq @ w_uq).reshape(n, t, MLA_HEADS, QK_NOPE + QK_ROPE)
    q_nope, q_pe = q[..., :QK_NOPE], q[..., QK_NOPE:]
    kv = h @ w_dkv
    c = rms_norm(kv[..., :KV_LORA], g_kv)
    cos, sin = rope_cos_sin(pos, QK_ROPE)
    q_pe = apply_rope(q_pe, cos[:, None], sin[:, None])
    k_pe = apply_rope(kv[..., KV_LORA:], cos, sin)
    q_lat = jnp.einsum('nthd,chd->nthc', q_nope, w_uk)
    q_full = jnp.concatenate([q_lat, q_pe], axis=-1)
    return q_full, c, k_pe


def mla_prompt_attend(q_full, c, k_pe):
    b, s = c.shape[:2]
    nb = s // ATTN_BLOCK
    keys = jnp.concatenate([c, k_pe], axis=-1)
    qb = q_full.reshape(b, nb, ATTN_BLOCK, MLA_HEADS, KV_LORA + QK_ROPE).transpose(1, 0, 2, 3, 4)
    key_pos = jnp.arange(s)

    def block(args):
        q_blk, j = args
        sc = jnp.einsum('bqhc,bsc->bhqs', q_blk, keys,
                        preferred_element_type=jnp.float32) * MLA_SCALE
        q_pos = j * ATTN_BLOCK + jnp.arange(ATTN_BLOCK)
        mask = key_pos[None, :] <= q_pos[:, None]
        p = jax.nn.softmax(jnp.where(mask, sc, -jnp.inf), axis=-1).astype(c.dtype)
        return jnp.einsum('bhqs,bsc->bqhc', p, c)

    o = lax.map(block, (qb, jnp.arange(nb)))
    return o.transpose(1, 0, 2, 3, 4).reshape(b, s, MLA_HEADS, KV_LORA)


def mla_sample_attend(q_full, c, k_pe, cache_ckv, cache_kpe, page_table):
    t = c.shape[1]
    page = cache_ckv.shape[1]
    past_len = page_table.shape[1] * page
    new_keys = jnp.concatenate([c, k_pe], axis=-1)
    key_idx = jnp.arange(past_len + t)[None, :]
    q_idx = jnp.arange(t)[:, None]
    mask = (key_idx < past_len) | (key_idx - past_len <= q_idx)

    def one_seq(args):
        q_s, nk_s, c_s, pages = args
        past_c = cache_ckv[pages].reshape(past_len, KV_LORA)
        past_pe = cache_kpe[pages].reshape(past_len, QK_ROPE)
        keys = jnp.concatenate([jnp.concatenate([past_c, past_pe], axis=-1), nk_s], axis=0)
        vals = jnp.concatenate([past_c, c_s], axis=0)
        sc = jnp.einsum('thc,sc->hts', q_s, keys, preferred_element_type=jnp.float32) * MLA_SCALE
        p = jax.nn.softmax(jnp.where(mask, sc, -jnp.inf), axis=-1).astype(vals.dtype)
        return jnp.einsum('hts,sc->thc', p, vals)

    return lax.map(one_seq, (q_full, new_keys, c, page_table))


def mla_out(o_lat, w_uv, w_o):
    n, t = o_lat.shape[:2]
    o = jnp.einsum('nthc,chv->nthv', o_lat, w_uv).reshape(n, t, MLA_HEADS * V_HEAD)
    return o @ w_o


def swiglu(h, w_gate, w_up, w_down):
    return (jax.nn.silu(h @ w_gate) * (h @ w_up)) @ w_down


def moe(h, w_router, b_router, w_gate, w_up, w_down):
    logits = h.astype(jnp.float32) @ w_router.astype(jnp.float32) + b_router.astype(jnp.float32)
    top_v, top_i = lax.top_k(logits, TOP_K)
    gates = jax.nn.softmax(top_v, axis=-1)
    combine = jnp.sum(jax.nn.one_hot(top_i, N_EXPERTS, dtype=jnp.float32) * gates[..., None], axis=-2)
    combine = combine.astype(h.dtype)
    out = jnp.zeros_like(h)
    for e in range(N_EXPERTS):
        out = out + combine[..., e:e + 1] * swiglu(h, w_gate[e], w_up[e], w_down[e])
    return out


def setup_inputs(seed: int = 0) -> dict:
    key = jax.random.key(seed)
    ks = iter(jax.random.split(key, 40))
    f32 = jnp.float32

    def w(shape, fan_in):
        return jax.random.normal(next(ks), shape, f32) * (fan_in ** -0.5)

    def gain(n):
        return 1.0 + 0.02 * jax.random.normal(next(ks), (n,), f32)

    n_pages = PAST_LEN // PAGE_SIZE
    n_used = DEC_BATCH * n_pages
    n_pool = n_used + n_used // 4
    page_table = jax.random.permutation(next(ks), n_pool)[:n_used].reshape(DEC_BATCH, n_pages).astype(jnp.int32)
    x_prompt = jax.random.normal(next(ks), (BATCH, SEQ, D_MODEL), f32)
    x_sample = jax.random.normal(next(ks), (DEC_BATCH, DEC_SEQ, D_MODEL), f32)
    cache_swa_k = jax.random.normal(next(ks), (DEC_BATCH, WINDOW, N_KV_HEADS, HEAD_DIM), f32)
    cache_swa_v = jax.random.normal(next(ks), (DEC_BATCH, WINDOW, N_KV_HEADS, HEAD_DIM), f32)
    cache_mla_ckv = jax.random.normal(next(ks), (n_pool, PAGE_SIZE, KV_LORA), f32)
    cache_mla_kpe = jax.random.normal(next(ks), (n_pool, PAGE_SIZE, QK_ROPE), f32)
    return {
        'x_prompt': x_prompt,
        'x_sample': x_sample,
        'cache_swa_k': cache_swa_k,
        'cache_swa_v': cache_swa_v,
        'cache_mla_ckv': cache_mla_ckv,
        'cache_mla_kpe': cache_mla_kpe,
        'page_table': page_table,
        'g_attn0': gain(D_MODEL),
        'w_qkv_swa': w((D_MODEL, (N_HEADS + 2 * N_KV_HEADS) * HEAD_DIM), D_MODEL),
        'sinks': 0.5 * jax.random.normal(next(ks), (N_HEADS,), f32),
        'w_o_swa': w((N_HEADS * HEAD_DIM, D_MODEL), N_HEADS * HEAD_DIM),
        'g_ffn0': gain(D_MODEL),
        'w_ffn_gate': w((D_MODEL, D_FF), D_MODEL),
        'w_ffn_up': w((D_MODEL, D_FF), D_MODEL),
        'w_ffn_down': w((D_FF, D_MODEL), D_FF),
        'g_attn1': gain(D_MODEL),
        'w_dq': w((D_MODEL, Q_LORA), D_MODEL),
        'g_q': gain(Q_LORA),
        'w_uq': w((Q_LORA, MLA_HEADS * (QK_NOPE + QK_ROPE)), Q_LORA),
        'w_dkv': w((D_MODEL, KV_LORA + QK_ROPE), D_MODEL),
        'g_kv': gain(KV_LORA),
        'w_uk': w((KV_LORA, MLA_HEADS, QK_NOPE), KV_LORA),
        'w_uv': w((KV_LORA, MLA_HEADS, V_HEAD), KV_LORA),
        'w_o_mla': w((MLA_HEADS * V_HEAD, D_MODEL), MLA_HEADS * V_HEAD),
        'g_ffn1': gain(D_MODEL),
        'w_router': w((D_MODEL, N_EXPERTS), D_MODEL),
        'b_router': 0.01 * jax.random.normal(next(ks), (N_EXPERTS,), f32),
        'w_exp_gate': w((N_EXPERTS, D_MODEL, D_FF), D_MODEL),
        'w_exp_up': w((N_EXPERTS, D_MODEL, D_FF), D_MODEL),
        'w_exp_down': w((N_EXPERTS, D_FF, D_MODEL), D_FF),
        'g_final': gain(D_MODEL),
    }


def reference(x_prompt, x_sample, cache_swa_k, cache_swa_v, cache_mla_ckv, cache_mla_kpe, page_table,
              g_attn0, w_qkv_swa, sinks, w_o_swa, g_ffn0, w_ffn_gate, w_ffn_up, w_ffn_down,
              g_attn1, w_dq, g_q, w_uq, w_dkv, g_kv, w_uk, w_uv, w_o_mla,
              g_ffn1, w_router, b_router, w_exp_gate, w_exp_up, w_exp_down, g_final):
    past_len = page_table.shape[1] * cache_mla_ckv.shape[1]
    pos_p = jnp.arange(x_prompt.shape[1])
    pos_s = past_len + jnp.arange(x_sample.shape[1])
    xp, xs = x_prompt, x_sample
    for i in range(DEPTH):
        if i % 2 == 0:
            hp, hs = rms_norm(xp, g_attn0), rms_norm(xs, g_attn0)
            qp, kp, vp = swa_qkv(hp, w_qkv_swa, pos_p)
            qs, k_s, v_s = swa_qkv(hs, w_qkv_swa, pos_s)
            xp = xp + swa_prompt(qp, kp, vp, sinks) @ w_o_swa
            a_s, swa_k_sample, swa_v_sample = swa_sample(qs, k_s, v_s, cache_swa_k, cache_swa_v, sinks, past_len)
            xs = xs + a_s @ w_o_swa
            swa_k_prompt, swa_v_prompt = kp[:, -WINDOW:], vp[:, -WINDOW:]
            xp = xp + swiglu(rms_norm(xp, g_ffn0), w_ffn_gate, w_ffn_up, w_ffn_down)
            xs = xs + swiglu(rms_norm(xs, g_ffn0), w_ffn_gate, w_ffn_up, w_ffn_down)
        else:
            hp, hs = rms_norm(xp, g_attn1), rms_norm(xs, g_attn1)
            qp, mla_ckv_prompt, mla_kpe_prompt = mla_project(hp, w_dq, g_q, w_uq, w_dkv, g_kv, w_uk, pos_p)
            qs, mla_ckv_sample, mla_kpe_sample = mla_project(hs, w_dq, g_q, w_uq, w_dkv, g_kv, w_uk, pos_s)
            o_p = mla_prompt_attend(qp, mla_ckv_prompt, mla_kpe_prompt)
            o_s = mla_sample_attend(qs, mla_ckv_sample, mla_kpe_sample, cache_mla_ckv, cache_mla_kpe, page_table)
            xp = xp + mla_out(o_p, w_uv, w_o_mla)
            xs = xs + mla_out(o_s, w_uv, w_o_mla)
            xp = xp + moe(rms_norm(xp, g_ffn1), w_router, b_router, w_exp_gate, w_exp_up, w_exp_down)
            xs = xs + moe(rms_norm(xs, g_ffn1), w_router, b_router, w_exp_gate, w_exp_up, w_exp_down)
    y_prompt = rms_norm(xp, g_final)
    y_sample = rms_norm(xs, g_final)
    return (y_prompt, y_sample, swa_k_prompt, swa_v_prompt, swa_k_sample, swa_v_sample,
            mla_ckv_prompt, mla_kpe_prompt, mla_ckv_sample, mla_kpe_sample)
```

```python
import functools

import jax
import jax.numpy as jnp
from jax import lax
from jax.experimental import pallas as pl
from jax.experimental.pallas import tpu as pltpu

F32 = jnp.float32
BF16 = jnp.bfloat16
I32 = jnp.int32

HEAD_DIM = 128
N_KV_HEADS = 8
GROUP = 4
WINDOW = 128
MLA_HEADS = 32
KV_LORA = 512
QK_NOPE = 128
QK_ROPE = 64
V_HEAD = 128
N_EXPERTS = 8
TOP_K = 2
ROPE_THETA = 10000.0
NORM_EPS = 1e-6
MLA_SCALE = (QK_NOPE + QK_ROPE) ** -0.5
SWA_SCALE = HEAD_DIM ** -0.5

LANES = 128
VMEM_LIMIT = 56 * 1024 * 1024
MOE_TILE = 512
PAGES_PER_STEP = 8
GATHER_ROWS = 256
COMBINE_ROWS = 128


def _pick(n, candidates):
    for c in candidates:
        if n % c == 0:
            return c
    return n


def _params(*sem):
    return pltpu.CompilerParams(dimension_semantics=sem, vmem_limit_bytes=VMEM_LIMIT)


def _dot_nt(a, b):
    return lax.dot_general(a, b, (((1,), (1,)), ((), ())), preferred_element_type=F32)


def _rms(x, g):
    return x * lax.rsqrt(jnp.mean(x * x, axis=-1, keepdims=True) + NORM_EPS) * g


def _rmsnorm_kernel(x_ref, g_ref, o_ref):
    o_ref[...] = _rms(x_ref[...].astype(F32), g_ref[...]).astype(o_ref.dtype)


def rmsnorm(x, g, out_dtype):
    t, d = x.shape
    bt = _pick(t, (512, 256, 128, 64, 32, 16, 8))
    return pl.pallas_call(
        _rmsnorm_kernel,
        grid=(t // bt,),
        in_specs=[pl.BlockSpec((bt, d), lambda i: (i, 0)), pl.BlockSpec((1, d), lambda i: (0, 0))],
        out_specs=pl.BlockSpec((bt, d), lambda i: (i, 0)),
        out_shape=jax.ShapeDtypeStruct((t, d), out_dtype),
        compiler_params=_params("parallel"),
        name="rmsnorm",
    )(x, g.reshape(1, d))


def _add_rmsnorm_kernel(x_ref, y_ref, g_ref, xo_ref, h_ref):
    x = x_ref[...] + y_ref[...]
    xo_ref[...] = x
    h_ref[...] = _rms(x, g_ref[...]).astype(h_ref.dtype)


def add_rmsnorm(x, y, g):
    t, d = x.shape
    bt = _pick(t, (256, 128, 64, 32, 16, 8))
    row = pl.BlockSpec((bt, d), lambda i: (i, 0))
    return pl.pallas_call(
        _add_rmsnorm_kernel,
        grid=(t // bt,),
        in_specs=[row, row, pl.BlockSpec((1, d), lambda i: (0, 0))],
        out_specs=[row, row],
        out_shape=[jax.ShapeDtypeStruct((t, d), F32), jax.ShapeDtypeStruct((t, d), BF16)],
        compiler_params=_params("parallel"),
        name="add_rmsnorm",
    )(x, y, g.reshape(1, d))


def _rope_tables(pos, dim):
    inv = jnp.power(jnp.float32(ROPE_THETA), -jnp.arange(0, dim, 2, dtype=F32) / dim)
    ang = pos.astype(F32)[:, None] * inv[None, :]
    cos, sin = jnp.cos(ang), jnp.sin(ang)
    reps = LANES // dim
    cos_t = jnp.tile(jnp.concatenate([cos, cos], axis=-1), (1, reps))
    sin_t = jnp.tile(jnp.concatenate([-sin, sin], axis=-1), (1, reps))
    return cos, sin, cos_t, sin_t


def _rope_lanes(x, cos_t, sin_t, half):
    if 2 * half == LANES:
        rot = pltpu.roll(x, half, 1)
    else:
        lane = lax.broadcasted_iota(I32, x.shape, 1)
        first = (lane % (2 * half)) < half
        rot = jnp.where(first, pltpu.roll(x, LANES - half, 1), pltpu.roll(x, half, 1))
    return x * cos_t + rot * sin_t


def _matmul_kernel(*refs, rope_half, n_rope_blocks):
    if rope_half:
        a_ref, w_ref, cos_ref, sin_ref, o_ref = refs
    else:
        a_ref, w_ref, o_ref = refs
    acc = jnp.dot(a_ref[...], w_ref[...].astype(BF16), preferred_element_type=F32)
    if not rope_half:
        o_ref[...] = acc.astype(o_ref.dtype)
        return
    j = pl.program_id(1)

    @pl.when(j < n_rope_blocks)
    def _():
        cos_t, sin_t = cos_ref[...], sin_ref[...]
        for c in range(acc.shape[1] // LANES):
            sl = slice(c * LANES, (c + 1) * LANES)
            o_ref[:, sl] = _rope_lanes(acc[:, sl], cos_t, sin_t, rope_half).astype(o_ref.dtype)

    @pl.when(j >= n_rope_blocks)
    def _():
        o_ref[...] = acc.astype(o_ref.dtype)


def matmul(a, w, *, out_dtype, rope=None, n_rope_cols=0):
    m, k = a.shape
    n = w.shape[1]
    bm = _pick(m, (1088, 1024, 512, 256, 128, 64, 32, 16, 8))
    bn = _pick(n, (256, 128)) if n % LANES == 0 else n
    in_specs = [pl.BlockSpec((bm, k), lambda i, j: (i, 0)), pl.BlockSpec((k, bn), lambda i, j: (0, j))]
    args = [a, w]
    rope_half = 0
    if rope is not None:
        cos_t, sin_t, rope_half = rope
        in_specs += [pl.BlockSpec((bm, LANES), lambda i, j: (i, 0))] * 2
        args += [cos_t, sin_t]
        assert n_rope_cols % bn == 0
    return pl.pallas_call(
        functools.partial(_matmul_kernel, rope_half=rope_half, n_rope_blocks=n_rope_cols // bn),
        grid=(m // bm, n // bn),
        in_specs=in_specs,
        out_specs=pl.BlockSpec((bm, bn), lambda i, j: (i, j)),
        out_shape=jax.ShapeDtypeStruct((m, n), out_dtype),
        compiler_params=_params("parallel", "arbitrary"),
        name="matmul",
    )(*args)


def _ffn_hidden_kernel(te_ref, nu_ref, a_ref, wg_ref, wu_ref, o_ref):
    m = pl.program_id(1)

    @pl.when(m < nu_ref[0])
    def _():
        a = a_ref[...]
        g = jnp.dot(a, wg_ref[0].astype(BF16), preferred_element_type=F32)
        u = jnp.dot(a, wu_ref[0].astype(BF16), preferred_element_type=F32)
        o_ref[...] = (g * jax.nn.sigmoid(g) * u).astype(o_ref.dtype)

    @pl.when(m >= nu_ref[0])
    def _():
        o_ref[...] = jnp.zeros_like(o_ref)


def _ffn_hidden(a, w_gate, w_up, tile_expert, n_used, tm):
    p, k = a.shape
    f = w_gate.shape[2]
    bn = _pick(f, (256, 128))
    w_spec = pl.BlockSpec((1, k, bn), lambda j, i, te, nu: (te[i], 0, j))
    return pl.pallas_call(
        _ffn_hidden_kernel,
        grid_spec=pltpu.PrefetchScalarGridSpec(
            num_scalar_prefetch=2,
            grid=(f // bn, p // tm),
            in_specs=[pl.BlockSpec((tm, k), lambda j, i, te, nu: (i, 0)), w_spec, w_spec],
            out_specs=pl.BlockSpec((tm, bn), lambda j, i, te, nu: (i, j)),
        ),
        out_shape=jax.ShapeDtypeStruct((p, f), BF16),
        compiler_params=_params("parallel", "arbitrary"),
        name="ffn_hidden",
    )(tile_expert, n_used, a, w_gate, w_up)


def _ffn_down_kernel(te_ref, nu_ref, a_ref, w_ref, o_ref):
    m, kk = pl.program_id(0), pl.program_id(2)

    @pl.when(kk == 0)
    def _():
        o_ref[...] = jnp.zeros_like(o_ref)

    @pl.when(m < nu_ref[0])
    def _():
        o_ref[...] += jnp.dot(a_ref[...], w_ref[0].astype(BF16), preferred_element_type=F32)


def _ffn_down(h, w_down, tile_expert, n_used, tm):
    p, f = h.shape
    d = w_down.shape[2]
    bn = _pick(d, (2048, 1024, 512, 256, 128))
    bk = _pick(f, (512, 256, 128))
    return pl.pallas_call(
        _ffn_down_kernel,
        grid_spec=pltpu.PrefetchScalarGridSpec(
            num_scalar_prefetch=2,
            grid=(p // tm, d // bn, f // bk),
            in_specs=[pl.BlockSpec((tm, bk), lambda i, j, kk, te, nu: (i, kk)),
                      pl.BlockSpec((1, bk, bn), lambda i, j, kk, te, nu: (te[i], kk, j))],
            out_specs=pl.BlockSpec((tm, bn), lambda i, j, kk, te, nu: (i, j)),
        ),
        out_shape=jax.ShapeDtypeStruct((p, d), F32),
        compiler_params=_params("parallel", "parallel", "arbitrary"),
        name="ffn_down",
    )(tile_expert, n_used, h, w_down)


def _dense_tiles(m):
    tm = _pick(m, (1088, 1024, 512, 256, 128, 64, 32, 16, 8))
    return tm, jnp.zeros((m // tm,), I32), jnp.full((1,), m // tm, I32)


def ffn_dense_hidden(a, w_gate, w_up):
    tm, te, nu = _dense_tiles(a.shape[0])
    return _ffn_hidden(a, w_gate[None], w_up[None], te, nu, tm)


def ffn_dense_down(h, w_down):
    tm, te, nu = _dense_tiles(h.shape[0])
    return _ffn_down(h, w_down[None], te, nu, tm)


def moe_hidden(a, w_gate, w_up, tile_expert, n_used):
    return _ffn_hidden(a, w_gate, w_up, tile_expert, n_used, min(MOE_TILE, a.shape[0]))


def moe_down(h, w_down, tile_expert, n_used):
    return _ffn_down(h, w_down, tile_expert, n_used, min(MOE_TILE, h.shape[0]))


def _sink_column(sinks_ref, kv, rows_per_head):
    r = lax.broadcasted_iota(I32, (GROUP * rows_per_head, 1), 0)
    col = jnp.full(r.shape, sinks_ref[kv * GROUP + GROUP - 1], F32)
    for g in range(GROUP - 2, -1, -1):
        col = jnp.where(r < (g + 1) * rows_per_head, sinks_ref[kv * GROUP + g], col)
    return col


def _swa_prompt_kernel(sinks_ref, q_ref, kc_ref, kp_ref, vc_ref, vp_ref, o_ref):
    j = pl.program_id(1)
    blk = q_ref.shape[0]
    rows = GROUP * blk
    tq = lax.broadcasted_iota(I32, (rows, 2 * blk), 0) % blk + blk
    ts = lax.broadcasted_iota(I32, (rows, 2 * blk), 1)
    first_key = jnp.where(j > 0, 0, blk)
    valid = (tq - ts >= 0) & (tq - ts <= WINDOW) & (ts >= first_key)
    for kv in range(N_KV_HEADS):
        ksl = slice(kv * HEAD_DIM, (kv + 1) * HEAD_DIM)
        kk = jnp.concatenate([kp_ref[:, ksl], kc_ref[:, ksl]], axis=0).astype(BF16)
        vv = jnp.concatenate([vp_ref[:, ksl], vc_ref[:, ksl]], axis=0).astype(BF16)
        q = jnp.concatenate(
            [q_ref[:, (kv * GROUP + g) * HEAD_DIM:(kv * GROUP + g + 1) * HEAD_DIM] for g in range(GROUP)],
            axis=0).astype(BF16)
        s = jnp.where(valid, _dot_nt(q, kk) * SWA_SCALE, -jnp.inf)
        sink = _sink_column(sinks_ref, kv, blk)
        mx = jnp.maximum(jnp.max(s, axis=-1, keepdims=True), sink)
        e = jnp.exp(s - mx)
        p = e / (jnp.sum(e, axis=-1, keepdims=True) + jnp.exp(sink - mx))
        o = jnp.dot(p.astype(BF16), vv, preferred_element_type=F32)
        for g in range(GROUP):
            hsl = slice((kv * GROUP + g) * HEAD_DIM, (kv * GROUP + g + 1) * HEAD_DIM)
            o_ref[:, hsl] = o[g * blk:(g + 1) * blk].astype(o_ref.dtype)


def swa_prompt_attention(qkv, sinks, batch, seq):
    blk = WINDOW
    nb = seq // blk
    dq = N_KV_HEADS * GROUP * HEAD_DIM
    dkv = N_KV_HEADS * HEAD_DIM
    kcol, vcol = dq // dkv, dq // dkv + 1
    cur = lambda b, j: b * nb + j
    prev = lambda b, j: b * nb + jnp.maximum(j - 1, 0)
    return pl.pallas_call(
        _swa_prompt_kernel,
        grid=(batch, nb),
        in_specs=[pl.BlockSpec(memory_space=pltpu.SMEM),
                  pl.BlockSpec((blk, dq), lambda b, j: (cur(b, j), 0)),
                  pl.BlockSpec((blk, dkv), lambda b, j: (cur(b, j), kcol)),
                  pl.BlockSpec((blk, dkv), lambda b, j: (prev(b, j), kcol)),
                  pl.BlockSpec((blk, dkv), lambda b, j: (cur(b, j), vcol)),
                  pl.BlockSpec((blk, dkv), lambda b, j: (prev(b, j), vcol))],
        out_specs=pl.BlockSpec((blk, dq), lambda b, j: (cur(b, j), 0)),
        out_shape=jax.ShapeDtypeStruct((batch * seq, dq), BF16),
        compiler_params=_params("parallel", "arbitrary"),
        name="swa_prompt",
    )(sinks, qkv, qkv, qkv, qkv, qkv)


def _swa_sample_kernel(sinks_ref, q_ref, kn_ref, vn_ref, kc_ref, vc_ref, o_ref):
    n_seq, n_new = kn_ref.shape[0], kn_ref.shape[1]
    rows = GROUP * n_new
    t = lax.broadcasted_iota(I32, (rows, WINDOW), 0) % n_new
    jc = lax.broadcasted_iota(I32, (rows, WINDOW), 1)
    valid_c = jc >= t
    t1 = lax.broadcasted_iota(I32, (rows, 1), 0) % n_new

    def one_seq(s, carry):
        for kv in range(N_KV_HEADS):
            ksl = slice(kv * HEAD_DIM, (kv + 1) * HEAD_DIM)
            q = q_ref[s, kv * rows:(kv + 1) * rows, :]
            qb = q.astype(BF16)
            s_c = jnp.where(valid_c, _dot_nt(qb, kc_ref[s, :, ksl].astype(BF16)) * SWA_SCALE, -jnp.inf)
            kn = kn_ref[s, :, ksl]
            vn = vn_ref[s, :, ksl]
            s_n = [jnp.where(t1 >= i, jnp.sum(q * kn[i:i + 1, :], axis=-1, keepdims=True) * SWA_SCALE, -jnp.inf)
                   for i in range(n_new)]
            sink = _sink_column(sinks_ref, kv, n_new)
            mx = jnp.maximum(jnp.max(s_c, axis=-1, keepdims=True), sink)
            for sn in s_n:
                mx = jnp.maximum(mx, sn)
            e_c = jnp.exp(s_c - mx)
            e_n = [jnp.exp(sn - mx) for sn in s_n]
            den = jnp.sum(e_c, axis=-1, keepdims=True) + jnp.exp(sink - mx)
            for en in e_n:
                den = den + en
            o = jnp.dot((e_c / den).astype(BF16), vc_ref[s, :, ksl].astype(BF16), preferred_element_type=F32)
            for i in range(n_new):
                o = o + (e_n[i] / den) * vn[i:i + 1, :]
            o_ref[s, kv * rows:(kv + 1) * rows, :] = o.astype(o_ref.dtype)
        return carry

    lax.fori_loop(0, n_seq, one_seq, 0)


def swa_sample_attention(q, k_new, v_new, cache_k, cache_v, sinks):
    n, t, dkv = k_new.shape
    sb = _pick(n, (8, 4, 2, 1))
    rows = q.shape[1]
    return pl.pallas_call(
        _swa_sample_kernel,
        grid=(n // sb,),
        in_specs=[pl.BlockSpec(memory_space=pltpu.SMEM),
                  pl.BlockSpec((sb, rows, HEAD_DIM), lambda i: (i, 0, 0)),
                  pl.BlockSpec((sb, t, dkv), lambda i: (i, 0, 0)),
                  pl.BlockSpec((sb, t, dkv), lambda i: (i, 0, 0)),
                  pl.BlockSpec((sb, WINDOW, dkv), lambda i: (i, 0, 0)),
                  pl.BlockSpec((sb, WINDOW, dkv), lambda i: (i, 0, 0))],
        out_specs=pl.BlockSpec((sb, rows, HEAD_DIM), lambda i: (i, 0, 0)),
        out_shape=jax.ShapeDtypeStruct((n, rows, HEAD_DIM), BF16),
        compiler_params=_params("parallel"),
        name="swa_sample",
    )(sinks, q, k_new, v_new, cache_k, cache_v)


def _mla_kv_post_kernel(kv_ref, g_ref, cos_ref, sin_ref, c_ref, pe_ref, cb_ref, peb_ref):
    x = kv_ref[...]
    c = _rms(x[:, :KV_LORA], g_ref[...])
    half = QK_ROPE // 2
    x1, x2 = x[:, KV_LORA:KV_LORA + half], x[:, KV_LORA + half:]
    cos, sin = cos_ref[...], sin_ref[...]
    pe = jnp.concatenate([x1 * cos - x2 * sin, x2 * cos + x1 * sin], axis=-1)
    c_ref[...] = c
    pe_ref[...] = pe
    cb_ref[...] = c.astype(BF16)
    peb_ref[...] = pe.astype(BF16)


def mla_kv_post(kv, g_kv, cos, sin):
    t = kv.shape[0]
    bt = _pick(t, (512, 256, 128, 64, 32, 16, 8))
    half = QK_ROPE // 2
    row = lambda w: pl.BlockSpec((bt, w), lambda i: (i, 0))
    return pl.pallas_call(
        _mla_kv_post_kernel,
        grid=(t // bt,),
        in_specs=[row(KV_LORA + QK_ROPE), pl.BlockSpec((1, KV_LORA), lambda i: (0, 0)), row(half), row(half)],
        out_specs=[row(KV_LORA), row(QK_ROPE), row(KV_LORA), row(QK_ROPE)],
        out_shape=[jax.ShapeDtypeStruct((t, KV_LORA), F32), jax.ShapeDtypeStruct((t, QK_ROPE), F32),
                   jax.ShapeDtypeStruct((t, KV_LORA), BF16), jax.ShapeDtypeStruct((t, QK_ROPE), BF16)],
        compiler_params=_params("parallel"),
        name="mla_kv_post",
    )(kv, g_kv.reshape(1, KV_LORA), cos, sin)


def _mla_absorb_kernel(qn_ref, qp_ref, w_ref, ql_ref, qph_ref):
    h = pl.program_id(0)
    ql_ref[0] = _dot_nt(qn_ref[...], w_ref[...].astype(BF16)).astype(ql_ref.dtype)
    pair = qp_ref[...]
    qph_ref[0] = jnp.where(h % 2 == 0, pair[:, :QK_ROPE], pair[:, QK_ROPE:])


def mla_absorb(q_nope, q_pe, w_uk):
    t = q_nope.shape[0]
    bm = _pick(t, (1088, 1024, 512, 256, 128, 64, 32, 16, 8))
    return pl.pallas_call(
        _mla_absorb_kernel,
        grid=(MLA_HEADS, t // bm),
        in_specs=[pl.BlockSpec((bm, QK_NOPE), lambda h, i: (i, h)),
                  pl.BlockSpec((bm, 2 * QK_ROPE), lambda h, i: (i, h // 2)),
                  pl.BlockSpec((KV_LORA, QK_NOPE), lambda h, i: (0, h))],
        out_specs=[pl.BlockSpec((1, bm, KV_LORA), lambda h, i: (h, i, 0)),
                   pl.BlockSpec((1, bm, QK_ROPE), lambda h, i: (h, i, 0))],
        out_shape=[jax.ShapeDtypeStruct((MLA_HEADS, t, KV_LORA), BF16),
                   jax.ShapeDtypeStruct((MLA_HEADS, t, QK_ROPE), BF16)],
        compiler_params=_params("parallel", "arbitrary"),
        name="mla_absorb",
    )(q_nope, q_pe, w_uk)


def _mla_uv_kernel(o_ref, w_ref, out_ref):
    out_ref[...] = jnp.dot(o_ref[0], w_ref[...].astype(BF16), preferred_element_type=F32).astype(out_ref.dtype)


def mla_uv(o_lat, w_uv):
    t = o_lat.shape[1]
    bm = _pick(t, (1024, 512, 256, 128, 64, 32, 16, 8))
    return pl.pallas_call(
        _mla_uv_kernel,
        grid=(MLA_HEADS, t // bm),
        in_specs=[pl.BlockSpec((1, bm, KV_LORA), lambda h, i: (h, i, 0)),
                  pl.BlockSpec((KV_LORA, V_HEAD), lambda h, i: (0, h))],
        out_specs=pl.BlockSpec((bm, V_HEAD), lambda h, i: (i, h)),
        out_shape=jax.ShapeDtypeStruct((t, MLA_HEADS * V_HEAD), BF16),
        compiler_params=_params("parallel", "arbitrary"),
        name="mla_uv",
    )(o_lat, w_uv)


def _mla_prompt_kernel(ql_ref, qp_ref, c_ref, pe_ref, o_ref, m_sc, l_sc, acc_sc):
    qb = pl.program_id(2)
    tq = ql_ref.shape[1]
    q, qp = ql_ref[0], qp_ref[0]
    m_sc[...] = jnp.full_like(m_sc, -jnp.inf)
    l_sc[...] = jnp.zeros_like(l_sc)
    acc_sc[...] = jnp.zeros_like(acc_sc)

    def chunk(i, diagonal):
        start = pl.multiple_of(i * tq, tq)
        kc = c_ref[pl.ds(start, tq), :]
        kp = pe_ref[pl.ds(start, tq), :]
        s = (_dot_nt(q, kc) + _dot_nt(qp, kp)) * MLA_SCALE
        if diagonal:
            row = lax.broadcasted_iota(I32, s.shape, 0)
            col = lax.broadcasted_iota(I32, s.shape, 1)
            s = jnp.where(col <= row, s, -jnp.inf)
        m_old = m_sc[...]
        m_new = jnp.maximum(m_old, jnp.max(s, axis=-1, keepdims=True))
        alpha = jnp.exp(m_old - m_new)
        p = jnp.exp(s - m_new)
        l_sc[...] = alpha * l_sc[...] + jnp.sum(p, axis=-1, keepdims=True)
        acc_sc[...] = alpha * acc_sc[...] + jnp.dot(p.astype(BF16), kc, preferred_element_type=F32)
        m_sc[...] = m_new

    def body(i, carry):
        chunk(i, False)
        return carry

    lax.fori_loop(0, qb, body, 0)
    chunk(qb, True)
    o_ref[0] = (acc_sc[...] / l_sc[...]).astype(o_ref.dtype)


def mla_prompt_attention(q_lat, q_pe, c, k_pe, batch, seq):
    tq = _pick(seq, (256, 128, 64, 32, 16))
    nq = seq // tq
    return pl.pallas_call(
        _mla_prompt_kernel,
        grid=(batch, MLA_HEADS, nq),
        in_specs=[pl.BlockSpec((1, tq, KV_LORA), lambda b, h, i: (h, b * nq + i, 0)),
                  pl.BlockSpec((1, tq, QK_ROPE), lambda b, h, i: (h, b * nq + i, 0)),
                  pl.BlockSpec((seq, KV_LORA), lambda b, h, i: (b, 0)),
                  pl.BlockSpec((seq, QK_ROPE), lambda b, h, i: (b, 0))],
        out_specs=pl.BlockSpec((1, tq, KV_LORA), lambda b, h, i: (h, b * nq + i, 0)),
        out_shape=jax.ShapeDtypeStruct((MLA_HEADS, batch * seq, KV_LORA), BF16),
        scratch_shapes=[pltpu.VMEM((tq, 1), F32), pltpu.VMEM((tq, 1), F32), pltpu.VMEM((tq, KV_LORA), F32)],
        compiler_params=_params("parallel", "parallel", "arbitrary"),
        name="mla_prompt",
    )(q_lat, q_pe, c, k_pe)


def _mla_sample_kernel(pt_ref, ql_ref, qp_ref, *refs, n_pages_step):
    ckv_refs = refs[:n_pages_step]
    kpe_refs = refs[n_pages_step:2 * n_pages_step]
    cn_ref, pn_ref, o_ref, m_sc, l_sc, acc_sc = refs[2 * n_pages_step:]
    g = pl.program_id(1)
    n_new = cn_ref.shape[1]

    @pl.when(g == 0)
    def _():
        m_sc[...] = jnp.full_like(m_sc, -jnp.inf)
        l_sc[...] = jnp.zeros_like(l_sc)
        acc_sc[...] = jnp.zeros_like(acc_sc)

    q, qp = ql_ref[0], qp_ref[0]
    keys = [r[0].astype(BF16) for r in ckv_refs]
    scores = [(_dot_nt(q, kc) + _dot_nt(qp, kp[0].astype(BF16))) * MLA_SCALE for kc, kp in zip(keys, kpe_refs)]
    m_old = m_sc[...]
    m_new = m_old
    for s in scores:
        m_new = jnp.maximum(m_new, jnp.max(s, axis=-1, keepdims=True))
    alpha = jnp.exp(m_old - m_new)
    l = alpha * l_sc[...]
    acc = alpha * acc_sc[...]
    for s, kc in zip(scores, keys):
        p = jnp.exp(s - m_new)
        l = l + jnp.sum(p, axis=-1, keepdims=True)
        acc = acc + jnp.dot(p.astype(BF16), kc, preferred_element_type=F32)
    m_sc[...] = m_new
    l_sc[...] = l
    acc_sc[...] = acc

    @pl.when(g == pl.num_programs(1) - 1)
    def _():
        qf, qpf = q.astype(F32), qp.astype(F32)
        cn, pn = cn_ref[0], pn_ref[0]
        t = lax.broadcasted_iota(I32, (q.shape[0], 1), 0) % n_new
        s_n = [jnp.where(t >= i,
                         (jnp.sum(qf * cn[i:i + 1, :], axis=-1, keepdims=True)
                          + jnp.sum(qpf * pn[i:i + 1, :], axis=-1, keepdims=True)) * MLA_SCALE,
                         -jnp.inf) for i in range(n_new)]
        m_old = m_sc[...]
        m_fin = m_old
        for s in s_n:
            m_fin = jnp.maximum(m_fin, s)
        alpha = jnp.exp(m_old - m_fin)
        l = alpha * l_sc[...]
        acc = alpha * acc_sc[...]
        for i, s in enumerate(s_n):
            p = jnp.exp(s - m_fin)
            l = l + p
            acc = acc + p * cn[i:i + 1, :]
        o_ref[0] = (acc / l).astype(o_ref.dtype)


def mla_sample_attention(q_lat, q_pe, cache_ckv, cache_kpe, page_table, c_new, pe_new):
    n, rows, _ = q_lat.shape
    n_pages = page_table.shape[1]
    page = cache_ckv.shape[1]
    t_new = c_new.shape[1]
    gp = _pick(n_pages, (PAGES_PER_STEP, 4, 2, 1))

    def page_spec(width, i):
        return pl.BlockSpec((1, page, width), lambda s, g, pt: (pt[s * n_pages + g * gp + i], 0, 0))

    seq_spec = lambda r, w: pl.BlockSpec((1, r, w), lambda s, g, pt: (s, 0, 0))
    return pl.pallas_call(
        functools.partial(_mla_sample_kernel, n_pages_step=gp),
        grid_spec=pltpu.PrefetchScalarGridSpec(
            num_scalar_prefetch=1,
            grid=(n, n_pages // gp),
            in_specs=([seq_spec(rows, KV_LORA), seq_spec(rows, QK_ROPE)]
                      + [page_spec(KV_LORA, i) for i in range(gp)]
                      + [page_spec(QK_ROPE, i) for i in range(gp)]
                      + [seq_spec(t_new, KV_LORA), seq_spec(t_new, QK_ROPE)]),
            out_specs=seq_spec(rows, KV_LORA),
            scratch_shapes=[pltpu.VMEM((rows, 1), F32), pltpu.VMEM((rows, 1), F32), pltpu.VMEM((rows, KV_LORA), F32)],
        ),
        out_shape=jax.ShapeDtypeStruct((n, rows, KV_LORA), BF16),
        compiler_params=_params("parallel", "arbitrary"),
        name="mla_sample",
    )(page_table.reshape(-1), q_lat, q_pe, *([cache_ckv] * gp), *([cache_kpe] * gp), c_new, pe_new)


def _router_kernel(x_ref, y_ref, g_ref, wt_ref, b_ref, xo_ref, h_ref, idx_ref, gate_ref):
    x = x_ref[...] + y_ref[...]
    xo_ref[...] = x
    h = _rms(x, g_ref[...])
    h_ref[...] = h
    logits = lax.dot_general(wt_ref[...], h, (((1,), (1,)), ((), ())), precision=lax.Precision.HIGHEST,
                             preferred_element_type=F32) + b_ref[...]
    e = lax.broadcasted_iota(I32, logits.shape, 0)
    v1 = jnp.max(logits, axis=0, keepdims=True)
    i1 = jnp.min(jnp.where(logits == v1, e, N_EXPERTS), axis=0, keepdims=True)
    rest = jnp.where(e == i1, -jnp.inf, logits)
    v2 = jnp.max(rest, axis=0, keepdims=True)
    i2 = jnp.min(jnp.where(rest == v2, e, N_EXPERTS), axis=0, keepdims=True)
    ex = jnp.exp(v2 - v1)
    idx_ref[...] = jnp.concatenate([i1, i2], axis=0)
    gate_ref[...] = jnp.concatenate([1.0 / (1.0 + ex), ex / (1.0 + ex)], axis=0)


def router(x, y, g, w_router, b_router):
    t, d = x.shape
    bt = _pick(t, (128,))
    row = pl.BlockSpec((bt, d), lambda i: (i, 0))
    top = pl.BlockSpec((TOP_K, bt), lambda i: (0, i))
    return pl.pallas_call(
        _router_kernel,
        grid=(t // bt,),
        in_specs=[row, row, pl.BlockSpec((1, d), lambda i: (0, 0)),
                  pl.BlockSpec((N_EXPERTS, d), lambda i: (0, 0)), pl.BlockSpec((N_EXPERTS, 1), lambda i: (0, 0))],
        out_specs=[row, row, top, top],
        out_shape=[jax.ShapeDtypeStruct((t, d), F32), jax.ShapeDtypeStruct((t, d), F32),
                   jax.ShapeDtypeStruct((TOP_K, t), I32), jax.ShapeDtypeStruct((TOP_K, t), F32)],
        compiler_params=_params("parallel"),
        name="router",
    )(x, y, g.reshape(1, d), w_router.T, b_router.reshape(N_EXPERTS, 1))


def _row_copy(src_ref, dst_ref, src_row, dst_row, sem):
    return pltpu.make_async_copy(src_ref.at[pl.ds(src_row, 1), :], dst_ref.at[pl.ds(dst_row, 1), :], sem)


def _gather_kernel(idx_ref, src_ref, o_ref, buf, sem):
    rows = buf.shape[0]
    base = pl.program_id(0) * rows

    def start(r, c):
        _row_copy(src_ref, buf, idx_ref[base + r], r, sem).start()
        return c

    def wait(r, c):
        _row_copy(src_ref, buf, 0, r, sem).wait()
        return c

    lax.fori_loop(0, rows, start, 0)
    lax.fori_loop(0, rows, wait, 0)
    o_ref[...] = buf[...].astype(o_ref.dtype)


def gather_rows_bf16(src, idx):
    p = idx.shape[0]
    d = src.shape[1]
    rows = _pick(p, (GATHER_ROWS, 128, 64, 32, 16))
    return pl.pallas_call(
        _gather_kernel,
        grid_spec=pltpu.PrefetchScalarGridSpec(
            num_scalar_prefetch=1,
            grid=(p // rows,),
            in_specs=[pl.BlockSpec(memory_space=pl.ANY)],
            out_specs=pl.BlockSpec((rows, d), lambda i, idx: (i, 0)),
            scratch_shapes=[pltpu.VMEM((rows, d), F32), pltpu.SemaphoreType.DMA(())],
        ),
        out_shape=jax.ShapeDtypeStruct((p, d), BF16),
        compiler_params=_params("arbitrary"),
        name="moe_dispatch",
    )(idx, src)


def _combine_kernel(pos_ref, x_ref, gate_ref, g_ref, y_ref, o_ref, buf, sem):
    rows = x_ref.shape[0]
    base = pl.program_id(0) * rows

    def start(r, c):
        for k in range(TOP_K):
            _row_copy(y_ref, buf, pos_ref[TOP_K * (base + r) + k], k * rows + r, sem).start()
        return c

    def wait(r, c):
        _row_copy(y_ref, buf, 0, r, sem).wait()
        return c

    lax.fori_loop(0, rows, start, 0)
    lax.fori_loop(0, TOP_K * rows, wait, 0)
    gates = gate_ref[...]
    x = x_ref[...]
    for k in range(TOP_K):
        x = x + gates[:, k:k + 1] * buf[k * rows:(k + 1) * rows, :]
    o_ref[...] = _rms(x, g_ref[...])


def moe_combine_norm(x, y, pos, gates, g_final):
    t, d = x.shape
    rows = _pick(t, (COMBINE_ROWS, 64, 32, 16, 8))
    return pl.pallas_call(
        _combine_kernel,
        grid_spec=pltpu.PrefetchScalarGridSpec(
            num_scalar_prefetch=1,
            grid=(t // rows,),
            in_specs=[pl.BlockSpec((rows, d), lambda i, pos: (i, 0)),
                      pl.BlockSpec((rows, TOP_K), lambda i, pos: (i, 0)),
                      pl.BlockSpec((1, d), lambda i, pos: (0, 0)),
                      pl.BlockSpec(memory_space=pl.ANY)],
            out_specs=pl.BlockSpec((rows, d), lambda i, pos: (i, 0)),
            scratch_shapes=[pltpu.VMEM((TOP_K * rows, d), F32), pltpu.SemaphoreType.DMA(())],
        ),
        out_shape=jax.ShapeDtypeStruct((t, d), F32),
        compiler_params=_params("arbitrary"),
        name="moe_combine",
    )(pos, x, gates, g_final.reshape(1, d), y)


def _routing_plan(top_idx, tm):
    t = top_idx.shape[1]
    n_pairs = t * TOP_K
    n_rows = (n_pairs // tm + N_EXPERTS) * tm
    pair_expert = top_idx.T.reshape(-1)
    order = jnp.argsort(pair_expert, stable=True).astype(I32)
    sorted_expert = pair_expert[order]
    counts = jnp.sum((pair_expert[None, :] == jnp.arange(N_EXPERTS, dtype=I32)[:, None]).astype(I32), axis=1)
    padded = (counts + tm - 1) // tm * tm
    group_end = jnp.cumsum(counts)
    padded_end = jnp.cumsum(padded)
    rank = jnp.arange(n_pairs, dtype=I32) - (group_end - counts)[sorted_expert]
    dest = ((padded_end - padded)[sorted_expert] + rank).astype(I32)
    row_token = jnp.zeros((n_rows,), I32).at[dest].set(order // TOP_K)
    pair_row = jnp.zeros((n_pairs,), I32).at[order].set(dest)
    n_used = (padded_end[-1] // tm).astype(I32)
    tile_start = jnp.arange(n_rows // tm, dtype=I32) * tm
    tile_expert = jnp.sum((tile_start[:, None] >= padded_end[None, :]).astype(I32), axis=1)
    tile_expert = jnp.where(tile_start // tm < n_used, tile_expert, tile_expert[jnp.maximum(n_used - 1, 0)]).astype(I32)
    return row_token, pair_row, tile_expert, n_used.reshape(1)


def kernel(x_prompt, x_sample, cache_swa_k, cache_swa_v, cache_mla_ckv, cache_mla_kpe, page_table,
           g_attn0, w_qkv_swa, sinks, w_o_swa, g_ffn0, w_ffn_gate, w_ffn_up, w_ffn_down,
           g_attn1, w_dq, g_q, w_uq, w_dkv, g_kv, w_uk, w_uv, w_o_mla,
           g_ffn1, w_router, b_router, w_exp_gate, w_exp_up, w_exp_down, g_final):
    batch, seq, d = x_prompt.shape
    n_seq, t_new, _ = x_sample.shape
    tp, ts = batch * seq, n_seq * t_new
    past_len = page_table.shape[1] * cache_mla_ckv.shape[1]
    n_heads = N_KV_HEADS * GROUP
    dq, dkv = n_heads * HEAD_DIM, N_KV_HEADS * HEAD_DIM

    x0 = jnp.concatenate([x_prompt.reshape(tp, d), x_sample.reshape(ts, d)], axis=0)
    pos = jnp.concatenate([jnp.tile(jnp.arange(seq), batch), jnp.tile(past_len + jnp.arange(t_new), n_seq)])
    _, _, cos_swa, sin_swa = _rope_tables(pos, HEAD_DIM)
    cos_pe, sin_pe, cos_mla, sin_mla = _rope_tables(pos, QK_ROPE)

    h = rmsnorm(x0, g_attn0, BF16)
    qkv = matmul(h, w_qkv_swa, out_dtype=F32, rope=(cos_swa, sin_swa, HEAD_DIM // 2), n_rope_cols=dq + dkv)
    attn_p = swa_prompt_attention(qkv, sinks, batch, seq)
    qkv_s = qkv[tp:]
    q_s = (qkv_s[:, :dq].reshape(n_seq, t_new, N_KV_HEADS, GROUP, HEAD_DIM)
           .transpose(0, 2, 3, 1, 4).reshape(n_seq, N_KV_HEADS * GROUP * t_new, HEAD_DIM))
    k_s = qkv_s[:, dq:dq + dkv].reshape(n_seq, t_new, dkv)
    v_s = qkv_s[:, dq + dkv:].reshape(n_seq, t_new, dkv)
    attn_s = swa_sample_attention(q_s, k_s, v_s, cache_swa_k.reshape(n_seq, WINDOW, dkv),
                                  cache_swa_v.reshape(n_seq, WINDOW, dkv), sinks)
    attn_s = (attn_s.reshape(n_seq, N_KV_HEADS, GROUP, t_new, HEAD_DIM)
              .transpose(0, 3, 1, 2, 4).reshape(ts, dq))
    y = matmul(jnp.concatenate([attn_p, attn_s], axis=0), w_o_swa, out_dtype=F32)

    qkv_p = qkv[:tp].reshape(batch, seq, dq + 2 * dkv)
    swa_k_prompt = qkv_p[:, seq - WINDOW:, dq:dq + dkv].reshape(batch, WINDOW, N_KV_HEADS, HEAD_DIM)
    swa_v_prompt = qkv_p[:, seq - WINDOW:, dq + dkv:].reshape(batch, WINDOW, N_KV_HEADS, HEAD_DIM)
    swa_k_sample = jnp.concatenate([cache_swa_k[:, t_new:], k_s.reshape(n_seq, t_new, N_KV_HEADS, HEAD_DIM)], axis=1)
    swa_v_sample = jnp.concatenate([cache_swa_v[:, t_new:], v_s.reshape(n_seq, t_new, N_KV_HEADS, HEAD_DIM)], axis=1)

    x1, h = add_rmsnorm(x0, y, g_ffn0)
    y = ffn_dense_down(ffn_dense_hidden(h, w_ffn_gate, w_ffn_up), w_ffn_down)

    x2, h = add_rmsnorm(x1, y, g_attn1)
    cq = rmsnorm(matmul(h, w_dq, out_dtype=F32), g_q, BF16)
    c, k_pe, c_b, k_pe_b = mla_kv_post(matmul(h, w_dkv, out_dtype=F32), g_kv, cos_pe, sin_pe)
    w_uq_h = w_uq.reshape(w_uq.shape[0], MLA_HEADS, QK_NOPE + QK_ROPE)
    q_nope = matmul(cq, w_uq_h[:, :, :QK_NOPE].reshape(-1, MLA_HEADS * QK_NOPE), out_dtype=BF16)
    q_pe = matmul(cq, w_uq_h[:, :, QK_NOPE:].reshape(-1, MLA_HEADS * QK_ROPE), out_dtype=BF16,
                  rope=(cos_mla, sin_mla, QK_ROPE // 2), n_rope_cols=MLA_HEADS * QK_ROPE)
    q_lat, q_pe_h = mla_absorb(q_nope, q_pe, w_uk.reshape(KV_LORA, MLA_HEADS * QK_NOPE))
    o_lat_p = mla_prompt_attention(q_lat, q_pe_h, c_b, k_pe_b, batch, seq)

    def per_seq(a):
        return a[:, tp:].reshape(MLA_HEADS, n_seq, t_new, -1).transpose(1, 0, 2, 3).reshape(n_seq, MLA_HEADS * t_new, -1)

    c_s = c[tp:].reshape(n_seq, t_new, KV_LORA)
    pe_s = k_pe[tp:].reshape(n_seq, t_new, QK_ROPE)
    o_lat_s = mla_sample_attention(per_seq(q_lat), per_seq(q_pe_h), cache_mla_ckv, cache_mla_kpe, page_table, c_s, pe_s)
    o_lat_s = (o_lat_s.reshape(n_seq, MLA_HEADS, t_new, KV_LORA).transpose(1, 0, 2, 3).reshape(MLA_HEADS, ts, KV_LORA))
    w_uv2 = w_uv.reshape(KV_LORA, MLA_HEADS * V_HEAD)
    o = jnp.concatenate([mla_uv(o_lat_p, w_uv2), mla_uv(o_lat_s, w_uv2)], axis=0)
    y = matmul(o, w_o_mla, out_dtype=F32)

    x3, hn, top_idx, gates = router(x2, y, g_ffn1, w_router, b_router)
    tm = min(MOE_TILE, tp + ts)
    row_token, pair_row, tile_expert, n_used = _routing_plan(top_idx, tm)
    a = gather_rows_bf16(hn, row_token)
    y = moe_down(moe_hidden(a, w_exp_gate, w_exp_up, tile_expert, n_used), w_exp_down, tile_expert, n_used)
    out = moe_combine_norm(x3, y, pair_row, gates.T, g_final)

    return (out[:tp].reshape(batch, seq, d), out[tp:].reshape(n_seq, t_new, d),
            swa_k_prompt, swa_v_prompt, swa_k_sample, swa_v_sample,
            c[:tp].reshape(batch, seq, KV_LORA), k_pe[:tp].reshape(batch, seq, QK_ROPE), c_s, pe_s)
```

```python
import functools
import math

import jax
import jax.numpy as jnp
from jax import lax
from jax.experimental import pallas as pl
from jax.experimental.pallas import tpu as pltpu

F32 = jnp.float32
BF16 = jnp.bfloat16
I32 = jnp.int32

HEAD_DIM = 128
N_KV_HEADS = 8
GROUP = 4
WINDOW = 128
MLA_HEADS = 32
KV_LORA = 512
QK_NOPE = 128
QK_ROPE = 64
V_HEAD = 128
N_EXPERTS = 8
TOP_K = 2
ROPE_THETA = 10000.0
NORM_EPS = 1e-6
MLA_SCALE = (QK_NOPE + QK_ROPE) ** -0.5
SWA_SCALE = HEAD_DIM ** -0.5

LANES = 128
VMEM_LIMIT = 56 * 1024 * 1024
MOE_TILE = 512
MOE_HIDDEN_BN = 512
DENSE_HIDDEN_BN = 256
PAGES_PER_STEP = 8
GATHER_ROWS = 256
COMBINE_ROWS = 128
DMA_LOOP_UNROLL = 8


def _pick(n, candidates):
    for c in candidates:
        if n % c == 0:
            return c
    return n


def _params(*sem):
    return pltpu.CompilerParams(dimension_semantics=sem, vmem_limit_bytes=VMEM_LIMIT)


def _dot_nt(a, b):
    return lax.dot_general(a, b, (((1,), (1,)), ((), ())), preferred_element_type=F32)


def _rms(x, g):
    return x * lax.rsqrt(jnp.mean(x * x, axis=-1, keepdims=True) + NORM_EPS) * g


def _rmsnorm_kernel(x_ref, g_ref, o_ref):
    o_ref[...] = _rms(x_ref[...].astype(F32), g_ref[...]).astype(o_ref.dtype)


def rmsnorm(x, g, out_dtype):
    t, d = x.shape
    bt = _pick(t, (512, 256, 128, 64, 32, 16, 8))
    return pl.pallas_call(
        _rmsnorm_kernel,
        grid=(t // bt,),
        in_specs=[pl.BlockSpec((bt, d), lambda i: (i, 0)), pl.BlockSpec((1, d), lambda i: (0, 0))],
        out_specs=pl.BlockSpec((bt, d), lambda i: (i, 0)),
        out_shape=jax.ShapeDtypeStruct((t, d), out_dtype),
        compiler_params=_params("parallel"),
        name="rmsnorm",
    )(x, g.reshape(1, d))


def _add_rmsnorm_kernel(x_ref, y_ref, g_ref, xo_ref, h_ref):
    x = x_ref[...] + y_ref[...]
    xo_ref[...] = x
    h_ref[...] = _rms(x, g_ref[...]).astype(h_ref.dtype)


def add_rmsnorm(x, y, g):
    t, d = x.shape
    bt = _pick(t, (256, 128, 64, 32, 16, 8))
    row = pl.BlockSpec((bt, d), lambda i: (i, 0))
    return pl.pallas_call(
        _add_rmsnorm_kernel,
        grid=(t // bt,),
        in_specs=[row, row, pl.BlockSpec((1, d), lambda i: (0, 0))],
        out_specs=[row, row],
        out_shape=[jax.ShapeDtypeStruct((t, d), F32), jax.ShapeDtypeStruct((t, d), BF16)],
        compiler_params=_params("parallel"),
        name="add_rmsnorm",
    )(x, y, g.reshape(1, d))


def _rope_tables(pos, dim):
    inv = jnp.power(jnp.float32(ROPE_THETA), -jnp.arange(0, dim, 2, dtype=F32) / dim)
    ang = pos.astype(F32)[:, None] * inv[None, :]
    cos, sin = jnp.cos(ang), jnp.sin(ang)
    reps = LANES // dim
    cos_t = jnp.tile(jnp.concatenate([cos, cos], axis=-1), (1, reps))
    sin_t = jnp.tile(jnp.concatenate([-sin, sin], axis=-1), (1, reps))
    return cos, sin, cos_t, sin_t


def _rope_lanes(x, cos_t, sin_t, half):
    if 2 * half == LANES:
        rot = pltpu.roll(x, half, 1)
    else:
        lane = lax.broadcasted_iota(I32, x.shape, 1)
        first = (lane % (2 * half)) < half
        rot = jnp.where(first, pltpu.roll(x, LANES - half, 1), pltpu.roll(x, half, 1))
    return x * cos_t + rot * sin_t


def _matmul_kernel(*refs, rope_half, n_rope_blocks):
    if rope_half:
        a_ref, w_ref, cos_ref, sin_ref, o_ref = refs
    else:
        a_ref, w_ref, o_ref = refs
    acc = jnp.dot(a_ref[...], w_ref[...].astype(BF16), preferred_element_type=F32)
    if not rope_half:
        o_ref[...] = acc.astype(o_ref.dtype)
        return
    j = pl.program_id(1)

    @pl.when(j < n_rope_blocks)
    def _():
        cos_t, sin_t = cos_ref[...], sin_ref[...]
        for c in range(acc.shape[1] // LANES):
            sl = slice(c * LANES, (c + 1) * LANES)
            o_ref[:, sl] = _rope_lanes(acc[:, sl], cos_t, sin_t, rope_half).astype(o_ref.dtype)

    @pl.when(j >= n_rope_blocks)
    def _():
        o_ref[...] = acc.astype(o_ref.dtype)


def matmul(a, w, *, out_dtype, rope=None, n_rope_cols=0):
    m, k = a.shape
    n = w.shape[1]
    bm = _pick(m, (1088, 1024, 512, 256, 128, 64, 32, 16, 8))
    bn = _pick(n, (256, 128)) if n % LANES == 0 else n
    in_specs = [pl.BlockSpec((bm, k), lambda i, j: (i, 0)), pl.BlockSpec((k, bn), lambda i, j: (0, j))]
    args = [a, w]
    rope_half = 0
    if rope is not None:
        cos_t, sin_t, rope_half = rope
        in_specs += [pl.BlockSpec((bm, LANES), lambda i, j: (i, 0))] * 2
        args += [cos_t, sin_t]
        assert n_rope_cols % bn == 0
    return pl.pallas_call(
        functools.partial(_matmul_kernel, rope_half=rope_half, n_rope_blocks=n_rope_cols // bn),
        grid=(m // bm, n // bn),
        in_specs=in_specs,
        out_specs=pl.BlockSpec((bm, bn), lambda i, j: (i, j)),
        out_shape=jax.ShapeDtypeStruct((m, n), out_dtype),
        compiler_params=_params("parallel", "arbitrary"),
        name="matmul",
    )(*args)


def _ffn_hidden_kernel(te_ref, nu_ref, a_ref, wg_ref, wu_ref, o_ref):
    m = pl.program_id(1)

    @pl.when(m < nu_ref[0])
    def _():
        a = a_ref[...]
        g = jnp.dot(a, wg_ref[0].astype(BF16), preferred_element_type=F32)
        u = jnp.dot(a, wu_ref[0].astype(BF16), preferred_element_type=F32)
        o_ref[...] = (g * jax.nn.sigmoid(g) * u).astype(o_ref.dtype)

    @pl.when(m >= nu_ref[0])
    def _():
        o_ref[...] = jnp.zeros_like(o_ref)


def _ffn_hidden(a, w_gate, w_up, tile_expert, n_used, tm, bn_max):
    p, k = a.shape
    f = w_gate.shape[2]
    bn = _pick(f, tuple(c for c in (512, 256, 128) if c <= bn_max))
    w_spec = pl.BlockSpec((1, k, bn), lambda j, i, te, nu: (te[i], 0, j))
    return pl.pallas_call(
        _ffn_hidden_kernel,
        grid_spec=pltpu.PrefetchScalarGridSpec(
            num_scalar_prefetch=2,
            grid=(f // bn, p // tm),
            in_specs=[pl.BlockSpec((tm, k), lambda j, i, te, nu: (i, 0)), w_spec, w_spec],
            out_specs=pl.BlockSpec((tm, bn), lambda j, i, te, nu: (i, j)),
        ),
        out_shape=jax.ShapeDtypeStruct((p, f), BF16),
        compiler_params=_params("parallel", "arbitrary"),
        name="ffn_hidden",
    )(tile_expert, n_used, a, w_gate, w_up)


def _ffn_down_kernel(te_ref, nu_ref, a_ref, w_ref, o_ref):
    m, kk = pl.program_id(0), pl.program_id(2)

    @pl.when(kk == 0)
    def _():
        o_ref[...] = jnp.zeros_like(o_ref)

    @pl.when(m < nu_ref[0])
    def _():
        o_ref[...] += jnp.dot(a_ref[...], w_ref[0].astype(BF16), preferred_element_type=F32)


def _ffn_down(h, w_down, tile_expert, n_used, tm):
    p, f = h.shape
    d = w_down.shape[2]
    bn = _pick(d, (2048, 1024, 512, 256, 128))
    bk = _pick(f, (1024, 512, 256, 128))
    return pl.pallas_call(
        _ffn_down_kernel,
        grid_spec=pltpu.PrefetchScalarGridSpec(
            num_scalar_prefetch=2,
            grid=(p // tm, d // bn, f // bk),
            in_specs=[pl.BlockSpec((tm, bk), lambda i, j, kk, te, nu: (i, kk)),
                      pl.BlockSpec((1, bk, bn), lambda i, j, kk, te, nu: (te[i], kk, j))],
            out_specs=pl.BlockSpec((tm, bn), lambda i, j, kk, te, nu: (i, j)),
        ),
        out_shape=jax.ShapeDtypeStruct((p, d), F32),
        compiler_params=_params("parallel", "parallel", "arbitrary"),
        name="ffn_down",
    )(tile_expert, n_used, h, w_down)


def _dense_tiles(m):
    tm = _pick(m, (1088, 1024, 512, 256, 128, 64, 32, 16, 8))
    return tm, jnp.zeros((m // tm,), I32), jnp.full((1,), m // tm, I32)


def ffn_dense_hidden(a, w_gate, w_up):
    tm, te, nu = _dense_tiles(a.shape[0])
    return _ffn_hidden(a, w_gate[None], w_up[None], te, nu, tm, DENSE_HIDDEN_BN)


def ffn_dense_down(h, w_down):
    tm, te, nu = _dense_tiles(h.shape[0])
    return _ffn_down(h, w_down[None], te, nu, tm)


def moe_hidden(a, w_gate, w_up, tile_expert, n_used):
    return _ffn_hidden(a, w_gate, w_up, tile_expert, n_used, min(MOE_TILE, a.shape[0]), MOE_HIDDEN_BN)


def moe_down(h, w_down, tile_expert, n_used):
    return _ffn_down(h, w_down, tile_expert, n_used, min(MOE_TILE, h.shape[0]))


def _sink_column(sinks_ref, kv, rows_per_head):
    r = lax.broadcasted_iota(I32, (GROUP * rows_per_head, 1), 0)
    col = jnp.full(r.shape, sinks_ref[kv * GROUP + GROUP - 1], F32)
    for g in range(GROUP - 2, -1, -1):
        col = jnp.where(r < (g + 1) * rows_per_head, sinks_ref[kv * GROUP + g], col)
    return col


def _swa_prompt_kernel(sinks_ref, q_ref, kc_ref, kp_ref, vc_ref, vp_ref, o_ref):
    j = pl.program_id(1)
    blk = q_ref.shape[0]
    rows = GROUP * blk
    tq = lax.broadcasted_iota(I32, (rows, 2 * blk), 0) % blk + blk
    ts = lax.broadcasted_iota(I32, (rows, 2 * blk), 1)
    first_key = jnp.where(j > 0, 0, blk)
    valid = (tq - ts >= 0) & (tq - ts <= WINDOW) & (ts >= first_key)
    for kv in range(N_KV_HEADS):
        ksl = slice(kv * HEAD_DIM, (kv + 1) * HEAD_DIM)
        kk = jnp.concatenate([kp_ref[:, ksl], kc_ref[:, ksl]], axis=0).astype(BF16)
        vv = jnp.concatenate([vp_ref[:, ksl], vc_ref[:, ksl]], axis=0).astype(BF16)
        q = jnp.concatenate(
            [q_ref[:, (kv * GROUP + g) * HEAD_DIM:(kv * GROUP + g + 1) * HEAD_DIM] for g in range(GROUP)],
            axis=0).astype(BF16)
        s = jnp.where(valid, _dot_nt(q, kk) * SWA_SCALE, -jnp.inf)
        sink = _sink_column(sinks_ref, kv, blk)
        mx = jnp.maximum(jnp.max(s, axis=-1, keepdims=True), sink)
        e = jnp.exp(s - mx)
        p = e / (jnp.sum(e, axis=-1, keepdims=True) + jnp.exp(sink - mx))
        o = jnp.dot(p.astype(BF16), vv, preferred_element_type=F32)
        for g in range(GROUP):
            hsl = slice((kv * GROUP + g) * HEAD_DIM, (kv * GROUP + g + 1) * HEAD_DIM)
            o_ref[:, hsl] = o[g * blk:(g + 1) * blk].astype(o_ref.dtype)


def swa_prompt_attention(qkv, sinks, batch, seq):
    blk = WINDOW
    nb = seq // blk
    dq = N_KV_HEADS * GROUP * HEAD_DIM
    dkv = N_KV_HEADS * HEAD_DIM
    kcol, vcol = dq // dkv, dq // dkv + 1
    cur = lambda b, j: b * nb + j
    prev = lambda b, j: b * nb + jnp.maximum(j - 1, 0)
    return pl.pallas_call(
        _swa_prompt_kernel,
        grid=(batch, nb),
        in_specs=[pl.BlockSpec(memory_space=pltpu.SMEM),
                  pl.BlockSpec((blk, dq), lambda b, j: (cur(b, j), 0)),
                  pl.BlockSpec((blk, dkv), lambda b, j: (cur(b, j), kcol)),
                  pl.BlockSpec((blk, dkv), lambda b, j: (prev(b, j), kcol)),
                  pl.BlockSpec((blk, dkv), lambda b, j: (cur(b, j), vcol)),
                  pl.BlockSpec((blk, dkv), lambda b, j: (prev(b, j), vcol))],
        out_specs=pl.BlockSpec((blk, dq), lambda b, j: (cur(b, j), 0)),
        out_shape=jax.ShapeDtypeStruct((batch * seq, dq), BF16),
        compiler_params=_params("parallel", "arbitrary"),
        name="swa_prompt",
    )(sinks, qkv, qkv, qkv, qkv, qkv)


def _swa_sample_kernel(sinks_ref, q_ref, kn_ref, vn_ref, kc_ref, vc_ref, o_ref):
    n_seq, n_new = kn_ref.shape[0], kn_ref.shape[1]
    rows = GROUP * n_new
    t = lax.broadcasted_iota(I32, (rows, WINDOW), 0) % n_new
    jc = lax.broadcasted_iota(I32, (rows, WINDOW), 1)
    valid_c = jc >= t
    t1 = lax.broadcasted_iota(I32, (rows, 1), 0) % n_new

    def one_seq(s, carry):
        for kv in range(N_KV_HEADS):
            ksl = slice(kv * HEAD_DIM, (kv + 1) * HEAD_DIM)
            q = q_ref[s, kv * rows:(kv + 1) * rows, :]
            qb = q.astype(BF16)
            s_c = jnp.where(valid_c, _dot_nt(qb, kc_ref[s, :, ksl].astype(BF16)) * SWA_SCALE, -jnp.inf)
            kn = kn_ref[s, :, ksl]
            vn = vn_ref[s, :, ksl]
            s_n = [jnp.where(t1 >= i, jnp.sum(q * kn[i:i + 1, :], axis=-1, keepdims=True) * SWA_SCALE, -jnp.inf)
                   for i in range(n_new)]
            sink = _sink_column(sinks_ref, kv, n_new)
            mx = jnp.maximum(jnp.max(s_c, axis=-1, keepdims=True), sink)
            for sn in s_n:
                mx = jnp.maximum(mx, sn)
            e_c = jnp.exp(s_c - mx)
            e_n = [jnp.exp(sn - mx) for sn in s_n]
            den = jnp.sum(e_c, axis=-1, keepdims=True) + jnp.exp(sink - mx)
            for en in e_n:
                den = den + en
            o = jnp.dot((e_c / den).astype(BF16), vc_ref[s, :, ksl].astype(BF16), preferred_element_type=F32)
            for i in range(n_new):
                o = o + (e_n[i] / den) * vn[i:i + 1, :]
            o_ref[s, kv * rows:(kv + 1) * rows, :] = o.astype(o_ref.dtype)
        return carry

    lax.fori_loop(0, n_seq, one_seq, 0)


def swa_sample_attention(q, k_new, v_new, cache_k, cache_v, sinks):
    n, t, dkv = k_new.shape
    sb = _pick(n, (8, 4, 2, 1))
    rows = q.shape[1]
    return pl.pallas_call(
        _swa_sample_kernel,
        grid=(n // sb,),
        in_specs=[pl.BlockSpec(memory_space=pltpu.SMEM),
                  pl.BlockSpec((sb, rows, HEAD_DIM), lambda i: (i, 0, 0)),
                  pl.BlockSpec((sb, t, dkv), lambda i: (i, 0, 0)),
                  pl.BlockSpec((sb, t, dkv), lambda i: (i, 0, 0)),
                  pl.BlockSpec((sb, WINDOW, dkv), lambda i: (i, 0, 0)),
                  pl.BlockSpec((sb, WINDOW, dkv), lambda i: (i, 0, 0))],
        out_specs=pl.BlockSpec((sb, rows, HEAD_DIM), lambda i: (i, 0, 0)),
        out_shape=jax.ShapeDtypeStruct((n, rows, HEAD_DIM), BF16),
        compiler_params=_params("parallel"),
        name="swa_sample",
    )(sinks, q, k_new, v_new, cache_k, cache_v)


def _mla_kv_post_kernel(kv_ref, g_ref, cos_ref, sin_ref, c_ref, pe_ref, cb_ref, peb_ref):
    x = kv_ref[...]
    c = _rms(x[:, :KV_LORA], g_ref[...])
    half = QK_ROPE // 2
    x1, x2 = x[:, KV_LORA:KV_LORA + half], x[:, KV_LORA + half:]
    cos, sin = cos_ref[...], sin_ref[...]
    pe = jnp.concatenate([x1 * cos - x2 * sin, x2 * cos + x1 * sin], axis=-1)
    c_ref[...] = c
    pe_ref[...] = pe
    cb_ref[...] = c.astype(BF16)
    peb_ref[...] = pe.astype(BF16)


def mla_kv_post(kv, g_kv, cos, sin):
    t = kv.shape[0]
    bt = _pick(t, (512, 256, 128, 64, 32, 16, 8))
    half = QK_ROPE // 2
    row = lambda w: pl.BlockSpec((bt, w), lambda i: (i, 0))
    return pl.pallas_call(
        _mla_kv_post_kernel,
        grid=(t // bt,),
        in_specs=[row(KV_LORA + QK_ROPE), pl.BlockSpec((1, KV_LORA), lambda i: (0, 0)), row(half), row(half)],
        out_specs=[row(KV_LORA), row(QK_ROPE), row(KV_LORA), row(QK_ROPE)],
        out_shape=[jax.ShapeDtypeStruct((t, KV_LORA), F32), jax.ShapeDtypeStruct((t, QK_ROPE), F32),
                   jax.ShapeDtypeStruct((t, KV_LORA), BF16), jax.ShapeDtypeStruct((t, QK_ROPE), BF16)],
        compiler_params=_params("parallel"),
        name="mla_kv_post",
    )(kv, g_kv.reshape(1, KV_LORA), cos, sin)


def _mla_absorb_kernel(qn_ref, qp_ref, w_ref, ql_ref, qph_ref):
    h = pl.program_id(0)
    ql_ref[0] = _dot_nt(qn_ref[...], w_ref[...].astype(BF16)).astype(ql_ref.dtype)
    pair = qp_ref[...]
    qph_ref[0] = jnp.where(h % 2 == 0, pair[:, :QK_ROPE], pair[:, QK_ROPE:])


def mla_absorb(q_nope, q_pe, w_uk):
    t = q_nope.shape[0]
    bm = _pick(t, (1088, 1024, 512, 256, 128, 64, 32, 16, 8))
    return pl.pallas_call(
        _mla_absorb_kernel,
        grid=(MLA_HEADS, t // bm),
        in_specs=[pl.BlockSpec((bm, QK_NOPE), lambda h, i: (i, h)),
                  pl.BlockSpec((bm, 2 * QK_ROPE), lambda h, i: (i, h // 2)),
                  pl.BlockSpec((KV_LORA, QK_NOPE), lambda h, i: (0, h))],
        out_specs=[pl.BlockSpec((1, bm, KV_LORA), lambda h, i: (h, i, 0)),
                   pl.BlockSpec((1, bm, QK_ROPE), lambda h, i: (h, i, 0))],
        out_shape=[jax.ShapeDtypeStruct((MLA_HEADS, t, KV_LORA), BF16),
                   jax.ShapeDtypeStruct((MLA_HEADS, t, QK_ROPE), BF16)],
        compiler_params=_params("parallel", "arbitrary"),
        name="mla_absorb",
    )(q_nope, q_pe, w_uk)


def _mla_uv_kernel(o_ref, w_ref, out_ref):
    out_ref[...] = jnp.dot(o_ref[0], w_ref[...].astype(BF16), preferred_element_type=F32).astype(out_ref.dtype)


def mla_uv(o_lat, w_uv):
    t = o_lat.shape[1]
    bm = _pick(t, (1024, 512, 256, 128, 64, 32, 16, 8))
    return pl.pallas_call(
        _mla_uv_kernel,
        grid=(MLA_HEADS, t // bm),
        in_specs=[pl.BlockSpec((1, bm, KV_LORA), lambda h, i: (h, i, 0)),
                  pl.BlockSpec((KV_LORA, V_HEAD), lambda h, i: (0, h))],
        out_specs=pl.BlockSpec((bm, V_HEAD), lambda h, i: (i, h)),
        out_shape=jax.ShapeDtypeStruct((t, MLA_HEADS * V_HEAD), BF16),
        compiler_params=_params("parallel", "arbitrary"),
        name="mla_uv",
    )(o_lat, w_uv)


def _mla_prompt_kernel(qn_ref, qp_ref, c_ref, pe_ref, wuk_ref, wuv_ref, o_ref, kn_sc, v_sc, *, tq):
    h = pl.program_id(1)
    seq = c_ref.shape[0]
    c = c_ref[...]
    kn_sc[...] = jnp.dot(c, wuk_ref[...].astype(BF16), preferred_element_type=F32).astype(BF16)
    v_sc[...] = jnp.dot(c, wuv_ref[...].astype(BF16), preferred_element_type=F32).astype(BF16)
    for qb in range(seq // tq):
        rows = slice(qb * tq, (qb + 1) * tq)
        n_keys = (qb + 1) * tq
        pair = qp_ref[rows, :]
        qp = jnp.where(h % 2 == 0, pair[:, :QK_ROPE], pair[:, QK_ROPE:])
        s = (_dot_nt(qn_ref[rows, :], kn_sc[:n_keys, :]) + _dot_nt(qp, pe_ref[:n_keys, :])) * MLA_SCALE
        row = lax.broadcasted_iota(I32, s.shape, 0) + qb * tq
        col = lax.broadcasted_iota(I32, s.shape, 1)
        s = jnp.where(col <= row, s, -jnp.inf)
        p = jnp.exp(s - jnp.max(s, axis=-1, keepdims=True))
        o = jnp.dot(p.astype(BF16), v_sc[:n_keys, :], preferred_element_type=F32)
        o_ref[rows, :] = (o / jnp.sum(p, axis=-1, keepdims=True)).astype(o_ref.dtype)


def mla_prompt_attention(q_nope, q_pe, c, k_pe, w_uk, w_uv, batch, seq):
    tq = _pick(seq, (256, 128, 64, 32, 16))
    head_cols = lambda w: pl.BlockSpec((seq, w), lambda b, h: (b, h))
    head_w = lambda w: pl.BlockSpec((KV_LORA, w), lambda b, h: (0, h))
    return pl.pallas_call(
        functools.partial(_mla_prompt_kernel, tq=tq),
        grid=(batch, MLA_HEADS),
        in_specs=[head_cols(QK_NOPE),
                  pl.BlockSpec((seq, 2 * QK_ROPE), lambda b, h: (b, h // 2)),
                  pl.BlockSpec((seq, KV_LORA), lambda b, h: (b, 0)),
                  pl.BlockSpec((seq, QK_ROPE), lambda b, h: (b, 0)),
                  head_w(QK_NOPE), head_w(V_HEAD)],
        out_specs=head_cols(V_HEAD),
        out_shape=jax.ShapeDtypeStruct((batch * seq, MLA_HEADS * V_HEAD), BF16),
        scratch_shapes=[pltpu.VMEM((seq, QK_NOPE), BF16), pltpu.VMEM((seq, V_HEAD), BF16)],
        compiler_params=_params("parallel", "arbitrary"),
        name="mla_prompt",
    )(q_nope, q_pe, c, k_pe, w_uk, w_uv)


def _mla_sample_kernel(pt_ref, ql_ref, qlt_ref, qp_ref, *refs, n_pages_step):
    ckv_refs = refs[:n_pages_step]
    kpet_refs = refs[n_pages_step:2 * n_pages_step]
    cn_ref, pn_ref, o_ref, m_sc, l_sc, acc_sc, key_sc = refs[2 * n_pages_step:]
    g = pl.program_id(1)
    n_new = cn_ref.shape[1]
    page = ckv_refs[0].shape[1]

    @pl.when(g == 0)
    def _():
        m_sc[...] = jnp.full_like(m_sc, -jnp.inf)
        l_sc[...] = jnp.zeros_like(l_sc)
        acc_sc[...] = jnp.zeros_like(acc_sc)

    q, qp = ql_ref[0], qp_ref[0]
    for i, r in enumerate(ckv_refs):
        key_sc[i * page:(i + 1) * page, :] = r[0].astype(BF16)
    keys = key_sc[...]
    s_lat_t = jnp.dot(keys, qlt_ref[0], preferred_element_type=F32) * MLA_SCALE
    pe_t = jnp.concatenate([r[0] for r in kpet_refs], axis=1).astype(BF16)
    s = s_lat_t.T + jnp.dot(qp, pe_t, preferred_element_type=F32) * MLA_SCALE
    m_old = m_sc[...]
    m_new = jnp.maximum(m_old, jnp.max(s, axis=-1, keepdims=True))
    alpha = jnp.exp(m_old - m_new)
    p = jnp.exp(s - m_new)
    l_sc[...] = alpha * l_sc[...] + jnp.sum(p, axis=-1, keepdims=True)
    acc_sc[...] = alpha * acc_sc[...] + jnp.dot(p.astype(BF16), keys, preferred_element_type=F32)
    m_sc[...] = m_new

    @pl.when(g == pl.num_programs(1) - 1)
    def _():
        qf, qpf = q.astype(F32), qp.astype(F32)
        cn, pn = cn_ref[0], pn_ref[0]
        t = lax.broadcasted_iota(I32, (q.shape[0], 1), 0) % n_new
        s_n = [jnp.where(t >= i,
                         (jnp.sum(qf * cn[i:i + 1, :], axis=-1, keepdims=True)
                          + jnp.sum(qpf * pn[i:i + 1, :], axis=-1, keepdims=True)) * MLA_SCALE,
                         -jnp.inf) for i in range(n_new)]
        m_old = m_sc[...]
        m_fin = m_old
        for s in s_n:
            m_fin = jnp.maximum(m_fin, s)
        alpha = jnp.exp(m_old - m_fin)
        l = alpha * l_sc[...]
        acc = alpha * acc_sc[...]
        for i, s in enumerate(s_n):
            p = jnp.exp(s - m_fin)
            l = l + p
            acc = acc + p * cn[i:i + 1, :]
        o_ref[0] = (acc / l).astype(o_ref.dtype)


def mla_sample_attention(q_lat, q_lat_t, q_pe, cache_ckv, cache_kpe_t, page_table, c_new, pe_new):
    n, rows, _ = q_lat.shape
    n_pages = page_table.shape[1]
    page = cache_ckv.shape[1]
    t_new = c_new.shape[1]
    gp = _pick(n_pages, (PAGES_PER_STEP, 8, 4, 2, 1))

    def page_spec(shape, i):
        return pl.BlockSpec((1,) + shape, lambda s, g, pt: (pt[s * n_pages + g * gp + i], 0, 0))

    seq_spec = lambda r, w: pl.BlockSpec((1, r, w), lambda s, g, pt: (s, 0, 0))
    return pl.pallas_call(
        functools.partial(_mla_sample_kernel, n_pages_step=gp),
        grid_spec=pltpu.PrefetchScalarGridSpec(
            num_scalar_prefetch=1,
            grid=(n, n_pages // gp),
            in_specs=([seq_spec(rows, KV_LORA), seq_spec(KV_LORA, rows), seq_spec(rows, QK_ROPE)]
                      + [page_spec((page, KV_LORA), i) for i in range(gp)]
                      + [page_spec((QK_ROPE, page), i) for i in range(gp)]
                      + [seq_spec(t_new, KV_LORA), seq_spec(t_new, QK_ROPE)]),
            out_specs=seq_spec(rows, KV_LORA),
            scratch_shapes=[pltpu.VMEM((rows, 1), F32), pltpu.VMEM((rows, 1), F32), pltpu.VMEM((rows, KV_LORA), F32),
                            pltpu.VMEM((gp * page, KV_LORA), BF16)],
        ),
        out_shape=jax.ShapeDtypeStruct((n, rows, KV_LORA), BF16),
        compiler_params=_params("parallel", "arbitrary"),
        name="mla_sample",
    )(page_table.reshape(-1), q_lat, q_lat_t, q_pe, *([cache_ckv] * gp), *([cache_kpe_t] * gp), c_new, pe_new)


def _router_kernel(x_ref, y_ref, g_ref, wt_ref, b_ref, xo_ref, h_ref, idx_ref, gate_ref):
    x = x_ref[...] + y_ref[...]
    xo_ref[...] = x
    h = _rms(x, g_ref[...])
    h_ref[...] = h
    logits = lax.dot_general(wt_ref[...], h, (((1,), (1,)), ((), ())), precision=lax.Precision.HIGHEST,
                             preferred_element_type=F32) + b_ref[...]
    e = lax.broadcasted_iota(I32, logits.shape, 0)
    v1 = jnp.max(logits, axis=0, keepdims=True)
    i1 = jnp.min(jnp.where(logits == v1, e, N_EXPERTS), axis=0, keepdims=True)
    rest = jnp.where(e == i1, -jnp.inf, logits)
    v2 = jnp.max(rest, axis=0, keepdims=True)
    i2 = jnp.min(jnp.where(rest == v2, e, N_EXPERTS), axis=0, keepdims=True)
    ex = jnp.exp(v2 - v1)
    idx_ref[...] = jnp.concatenate([i1, i2], axis=0)
    gate_ref[...] = jnp.concatenate([1.0 / (1.0 + ex), ex / (1.0 + ex)], axis=0)


def router(x, y, g, w_router, b_router):
    t, d = x.shape
    bt = _pick(t, (128,))
    row = pl.BlockSpec((bt, d), lambda i: (i, 0))
    top = pl.BlockSpec((TOP_K, bt), lambda i: (0, i))
    return pl.pallas_call(
        _router_kernel,
        grid=(t // bt,),
        in_specs=[row, row, pl.BlockSpec((1, d), lambda i: (0, 0)),
                  pl.BlockSpec((N_EXPERTS, d), lambda i: (0, 0)), pl.BlockSpec((N_EXPERTS, 1), lambda i: (0, 0))],
        out_specs=[row, row, top, top],
        out_shape=[jax.ShapeDtypeStruct((t, d), F32), jax.ShapeDtypeStruct((t, d), F32),
                   jax.ShapeDtypeStruct((TOP_K, t), I32), jax.ShapeDtypeStruct((TOP_K, t), F32)],
        compiler_params=_params("parallel"),
        name="router",
    )(x, y, g.reshape(1, d), w_router.T, b_router.reshape(N_EXPERTS, 1))


def _row_copy(src_ref, dst_ref, src_row, dst_row, sem):
    return pltpu.make_async_copy(src_ref.at[pl.ds(src_row, 1), :], dst_ref.at[pl.ds(dst_row, 1), :], sem)


def _gather_kernel(idx_ref, src_ref, o_ref, buf, sem):
    rows = buf.shape[0]
    base = pl.program_id(0) * rows

    def start(r, c):
        _row_copy(src_ref, buf, idx_ref[base + r], r, sem).start()
        return c

    def wait(r, c):
        _row_copy(src_ref, buf, 0, r, sem).wait()
        return c

    lax.fori_loop(0, rows, start, 0, unroll=DMA_LOOP_UNROLL)
    lax.fori_loop(0, rows, wait, 0, unroll=DMA_LOOP_UNROLL)
    o_ref[...] = buf[...].astype(o_ref.dtype)


def gather_rows_bf16(src, idx):
    p = idx.shape[0]
    d = src.shape[1]
    rows = _pick(p, (GATHER_ROWS, 128, 64, 32, 16))
    return pl.pallas_call(
        _gather_kernel,
        grid_spec=pltpu.PrefetchScalarGridSpec(
            num_scalar_prefetch=1,
            grid=(p // rows,),
            in_specs=[pl.BlockSpec(memory_space=pl.ANY)],
            out_specs=pl.BlockSpec((rows, d), lambda i, idx: (i, 0)),
            scratch_shapes=[pltpu.VMEM((rows, d), F32), pltpu.SemaphoreType.DMA(())],
        ),
        out_shape=jax.ShapeDtypeStruct((p, d), BF16),
        compiler_params=_params("arbitrary"),
        name="moe_dispatch",
    )(idx, src)


def _combine_kernel(pos_ref, x_ref, gate_ref, g_ref, y_ref, oa_ref, ob_ref, buf, sem, *, n_a):
    rows = x_ref.shape[0]
    step = pl.program_id(0)
    base = step * rows

    def start(r, c):
        for k in range(TOP_K):
            _row_copy(y_ref, buf, pos_ref[TOP_K * (base + r) + k], k * rows + r, sem).start()
        return c

    def wait(r, c):
        _row_copy(y_ref, buf, 0, r, sem).wait()
        return c

    lax.fori_loop(0, rows, start, 0, unroll=DMA_LOOP_UNROLL)
    lax.fori_loop(0, TOP_K * rows, wait, 0, unroll=DMA_LOOP_UNROLL)
    gates = gate_ref[...]
    x = x_ref[...]
    for k in range(TOP_K):
        x = x + gates[:, k:k + 1] * buf[k * rows:(k + 1) * rows, :]
    out = _rms(x, g_ref[...])

    @pl.when(step < n_a)
    def _():
        oa_ref[...] = out

    @pl.when(step >= n_a)
    def _():
        ob_ref[...] = out


def moe_combine_norm(x, y, pos, gates, g_final, t_a):
    t, d = x.shape
    assert 0 < t_a < t
    rows = _pick(math.gcd(t_a, t - t_a), (COMBINE_ROWS, 64, 32, 16, 8))
    n_a = t_a // rows
    return pl.pallas_call(
        functools.partial(_combine_kernel, n_a=n_a),
        grid_spec=pltpu.PrefetchScalarGridSpec(
            num_scalar_prefetch=1,
            grid=(t // rows,),
            in_specs=[pl.BlockSpec((rows, d), lambda i, pos: (i, 0)),
                      pl.BlockSpec((rows, TOP_K), lambda i, pos: (i, 0)),
                      pl.BlockSpec((1, d), lambda i, pos: (0, 0)),
                      pl.BlockSpec(memory_space=pl.ANY)],
            out_specs=[pl.BlockSpec((rows, d), lambda i, pos: (jnp.minimum(i, n_a - 1), 0)),
                       pl.BlockSpec((rows, d), lambda i, pos: (jnp.maximum(i - n_a, 0), 0))],
            scratch_shapes=[pltpu.VMEM((TOP_K * rows, d), F32), pltpu.SemaphoreType.DMA(())],
        ),
        out_shape=[jax.ShapeDtypeStruct((t_a, d), F32), jax.ShapeDtypeStruct((t - t_a, d), F32)],
        compiler_params=_params("arbitrary"),
        name="moe_combine",
    )(pos, x, gates, g_final.reshape(1, d), y)


def _routing_plan(top_idx, tm):
    t = top_idx.shape[1]
    n_pairs = t * TOP_K
    n_rows = (n_pairs // tm + N_EXPERTS) * tm
    pair_expert = top_idx.T.reshape(-1)
    order = jnp.argsort(pair_expert, stable=True).astype(I32)
    sorted_expert = pair_expert[order]
    counts = jnp.sum((pair_expert[None, :] == jnp.arange(N_EXPERTS, dtype=I32)[:, None]).astype(I32), axis=1)
    padded = (counts + tm - 1) // tm * tm
    group_end = jnp.cumsum(counts)
    padded_end = jnp.cumsum(padded)
    rank = jnp.arange(n_pairs, dtype=I32) - (group_end - counts)[sorted_expert]
    dest = ((padded_end - padded)[sorted_expert] + rank).astype(I32)
    row_token = jnp.zeros((n_rows,), I32).at[dest].set(order // TOP_K)
    pair_row = jnp.zeros((n_pairs,), I32).at[order].set(dest)
    n_used = (padded_end[-1] // tm).astype(I32)
    tile_start = jnp.arange(n_rows // tm, dtype=I32) * tm
    tile_expert = jnp.sum((tile_start[:, None] >= padded_end[None, :]).astype(I32), axis=1)
    tile_expert = jnp.where(tile_start // tm < n_used, tile_expert, tile_expert[jnp.maximum(n_used - 1, 0)]).astype(I32)
    return row_token, pair_row, tile_expert, n_used.reshape(1)


def kernel(x_prompt, x_sample, cache_swa_k, cache_swa_v, cache_mla_ckv, cache_mla_kpe, page_table,
           g_attn0, w_qkv_swa, sinks, w_o_swa, g_ffn0, w_ffn_gate, w_ffn_up, w_ffn_down,
           g_attn1, w_dq, g_q, w_uq, w_dkv, g_kv, w_uk, w_uv, w_o_mla,
           g_ffn1, w_router, b_router, w_exp_gate, w_exp_up, w_exp_down, g_final):
    batch, seq, d = x_prompt.shape
    n_seq, t_new, _ = x_sample.shape
    tp, ts = batch * seq, n_seq * t_new
    past_len = page_table.shape[1] * cache_mla_ckv.shape[1]
    n_heads = N_KV_HEADS * GROUP
    dq, dkv = n_heads * HEAD_DIM, N_KV_HEADS * HEAD_DIM

    x0 = jnp.concatenate([x_prompt.reshape(tp, d), x_sample.reshape(ts, d)], axis=0)
    pos = jnp.concatenate([jnp.tile(jnp.arange(seq), batch), jnp.tile(past_len + jnp.arange(t_new), n_seq)])
    _, _, cos_swa, sin_swa = _rope_tables(pos, HEAD_DIM)
    cos_pe, sin_pe, cos_mla, sin_mla = _rope_tables(pos, QK_ROPE)

    h = rmsnorm(x0, g_attn0, BF16)
    qkv = matmul(h, w_qkv_swa, out_dtype=F32, rope=(cos_swa, sin_swa, HEAD_DIM // 2), n_rope_cols=dq + dkv)
    attn_p = swa_prompt_attention(qkv, sinks, batch, seq)
    qkv_s = qkv[tp:]
    q_s = (qkv_s[:, :dq].reshape(n_seq, t_new, N_KV_HEADS, GROUP, HEAD_DIM)
           .transpose(0, 2, 3, 1, 4).reshape(n_seq, N_KV_HEADS * GROUP * t_new, HEAD_DIM))
    k_s = qkv_s[:, dq:dq + dkv].reshape(n_seq, t_new, dkv)
    v_s = qkv_s[:, dq + dkv:].reshape(n_seq, t_new, dkv)
    attn_s = swa_sample_attention(q_s, k_s, v_s, cache_swa_k.reshape(n_seq, WINDOW, dkv),
                                  cache_swa_v.reshape(n_seq, WINDOW, dkv), sinks)
    attn_s = (attn_s.reshape(n_seq, N_KV_HEADS, GROUP, t_new, HEAD_DIM)
              .transpose(0, 3, 1, 2, 4).reshape(ts, dq))
    y = matmul(jnp.concatenate([attn_p, attn_s], axis=0), w_o_swa, out_dtype=F32)

    qkv_p = qkv[:tp].reshape(batch, seq, dq + 2 * dkv)
    swa_k_prompt = qkv_p[:, seq - WINDOW:, dq:dq + dkv].reshape(batch, WINDOW, N_KV_HEADS, HEAD_DIM)
    swa_v_prompt = qkv_p[:, seq - WINDOW:, dq + dkv:].reshape(batch, WINDOW, N_KV_HEADS, HEAD_DIM)
    swa_k_sample = jnp.concatenate([cache_swa_k[:, t_new:], k_s.reshape(n_seq, t_new, N_KV_HEADS, HEAD_DIM)], axis=1)
    swa_v_sample = jnp.concatenate([cache_swa_v[:, t_new:], v_s.reshape(n_seq, t_new, N_KV_HEADS, HEAD_DIM)], axis=1)

    x1, h = add_rmsnorm(x0, y, g_ffn0)
    y = ffn_dense_down(ffn_dense_hidden(h, w_ffn_gate, w_ffn_up), w_ffn_down)

    x2, h = add_rmsnorm(x1, y, g_attn1)
    cq = rmsnorm(matmul(h, w_dq, out_dtype=F32), g_q, BF16)
    c, k_pe, c_b, k_pe_b = mla_kv_post(matmul(h, w_dkv, out_dtype=F32), g_kv, cos_pe, sin_pe)
    w_uq_h = w_uq.reshape(w_uq.shape[0], MLA_HEADS, QK_NOPE + QK_ROPE)
    q_nope = matmul(cq, w_uq_h[:, :, :QK_NOPE].reshape(-1, MLA_HEADS * QK_NOPE), out_dtype=BF16)
    q_pe = matmul(cq, w_uq_h[:, :, QK_NOPE:].reshape(-1, MLA_HEADS * QK_ROPE), out_dtype=BF16,
                  rope=(cos_mla, sin_mla, QK_ROPE // 2), n_rope_cols=MLA_HEADS * QK_ROPE)
    w_uk2 = w_uk.reshape(KV_LORA, MLA_HEADS * QK_NOPE)
    w_uv2 = w_uv.reshape(KV_LORA, MLA_HEADS * V_HEAD)
    o_p = mla_prompt_attention(q_nope, q_pe, c_b, k_pe_b, w_uk2, w_uv2, batch, seq)

    q_lat, q_pe_h = mla_absorb(q_nope[tp:], q_pe[tp:], w_uk2)

    def per_seq(a):
        return a.reshape(MLA_HEADS, n_seq, t_new, -1).transpose(1, 0, 2, 3).reshape(n_seq, MLA_HEADS * t_new, -1)

    c_s = c[tp:].reshape(n_seq, t_new, KV_LORA)
    pe_s = k_pe[tp:].reshape(n_seq, t_new, QK_ROPE)
    q_lat_s = per_seq(q_lat)
    o_lat_s = mla_sample_attention(q_lat_s, q_lat_s.transpose(0, 2, 1), per_seq(q_pe_h), cache_mla_ckv,
                                   cache_mla_kpe.transpose(0, 2, 1), page_table, c_s, pe_s)
    o_lat_s = (o_lat_s.reshape(n_seq, MLA_HEADS, t_new, KV_LORA).transpose(1, 0, 2, 3).reshape(MLA_HEADS, ts, KV_LORA))
    o = jnp.concatenate([o_p, mla_uv(o_lat_s, w_uv2)], axis=0)
    y = matmul(o, w_o_mla, out_dtype=F32)

    x3, hn, top_idx, gates = router(x2, y, g_ffn1, w_router, b_router)
    tm = min(MOE_TILE, tp + ts)
    row_token, pair_row, tile_expert, n_used = _routing_plan(top_idx, tm)
    a = gather_rows_bf16(hn, row_token)
    y = moe_down(moe_hidden(a, w_exp_gate, w_exp_up, tile_expert, n_used), w_exp_down, tile_expert, n_used)
    out_p, out_s = moe_combine_norm(x3, y, pair_row, gates.T, g_final, tp)

    return (out_p.reshape(batch, seq, d), out_s.reshape(n_seq, t_new, d),
            swa_k_prompt, swa_v_prompt, swa_k_sample, swa_v_sample,
            c[:tp].reshape(batch, seq, KV_LORA), k_pe[:tp].reshape(batch, seq, QK_ROPE), c_s, pe_s)
```

```python
import functools
import math

import jax
import jax.numpy as jnp
from jax import lax
from jax.experimental import pallas as pl
from jax.experimental.pallas import tpu as pltpu

F32 = jnp.float32
BF16 = jnp.bfloat16
I32 = jnp.int32

HEAD_DIM = 128
N_KV_HEADS = 8
GROUP = 4
WINDOW = 128
MLA_HEADS = 32
KV_LORA = 512
QK_NOPE = 128
QK_ROPE = 64
V_HEAD = 128
N_EXPERTS = 8
TOP_K = 2
ROPE_THETA = 10000.0
NORM_EPS = 1e-6
MLA_SCALE = (QK_NOPE + QK_ROPE) ** -0.5
RUNNING_MAX_INIT = float(jnp.finfo(jnp.float32).min)
SWA_SCALE = HEAD_DIM ** -0.5

LANES = 128
VMEM_LIMIT = 56 * 1024 * 1024
MOE_TILE = 512
MOE_DOWN_TILE = 1024
MOE_HIDDEN_BN = 512
DENSE_HIDDEN_BN = 256
PAGES_PER_STEP = 8
GATHER_ROWS = 256
COMBINE_ROWS = 128
DMA_LOOP_UNROLL = 8


def _pick(n, candidates):
    for c in candidates:
        if n % c == 0:
            return c
    return n


def _params(*sem):
    return pltpu.CompilerParams(dimension_semantics=sem, vmem_limit_bytes=VMEM_LIMIT)


def _dot_nt(a, b):
    return lax.dot_general(a, b, (((1,), (1,)), ((), ())), preferred_element_type=F32)


def _rms(x, g):
    return x * lax.rsqrt(jnp.mean(x * x, axis=-1, keepdims=True) + NORM_EPS) * g


def _rmsnorm_kernel(x_ref, g_ref, o_ref):
    o_ref[...] = _rms(x_ref[...].astype(F32), g_ref[...]).astype(o_ref.dtype)


def rmsnorm(x, g, out_dtype):
    t, d = x.shape
    bt = _pick(t, (512, 256, 128, 64, 32, 16, 8))
    return pl.pallas_call(
        _rmsnorm_kernel,
        grid=(t // bt,),
        in_specs=[pl.BlockSpec((bt, d), lambda i: (i, 0)), pl.BlockSpec((1, d), lambda i: (0, 0))],
        out_specs=pl.BlockSpec((bt, d), lambda i: (i, 0)),
        out_shape=jax.ShapeDtypeStruct((t, d), out_dtype),
        compiler_params=_params("parallel"),
        name="rmsnorm",
    )(x, g.reshape(1, d))


def _add_rmsnorm_kernel(x_ref, y_ref, g_ref, xo_ref, h_ref):
    x = x_ref[...] + y_ref[...]
    xo_ref[...] = x
    h_ref[...] = _rms(x, g_ref[...]).astype(h_ref.dtype)


def add_rmsnorm(x, y, g):
    t, d = x.shape
    bt = _pick(t, (256, 128, 64, 32, 16, 8))
    row = pl.BlockSpec((bt, d), lambda i: (i, 0))
    return pl.pallas_call(
        _add_rmsnorm_kernel,
        grid=(t // bt,),
        in_specs=[row, row, pl.BlockSpec((1, d), lambda i: (0, 0))],
        out_specs=[row, row],
        out_shape=[jax.ShapeDtypeStruct((t, d), F32), jax.ShapeDtypeStruct((t, d), BF16)],
        compiler_params=_params("parallel"),
        name="add_rmsnorm",
    )(x, y, g.reshape(1, d))


def _rope_tables(pos, dim):
    inv = jnp.power(jnp.float32(ROPE_THETA), -jnp.arange(0, dim, 2, dtype=F32) / dim)
    ang = pos.astype(F32)[:, None] * inv[None, :]
    cos, sin = jnp.cos(ang), jnp.sin(ang)
    reps = LANES // dim
    cos_t = jnp.tile(jnp.concatenate([cos, cos], axis=-1), (1, reps))
    sin_t = jnp.tile(jnp.concatenate([-sin, sin], axis=-1), (1, reps))
    return cos, sin, cos_t, sin_t


def _rope_lanes(x, cos_t, sin_t, half):
    if 2 * half == LANES:
        rot = pltpu.roll(x, half, 1)
    else:
        lane = lax.broadcasted_iota(I32, x.shape, 1)
        first = (lane % (2 * half)) < half
        rot = jnp.where(first, pltpu.roll(x, LANES - half, 1), pltpu.roll(x, half, 1))
    return x * cos_t + rot * sin_t


def _matmul_kernel(*refs, rope_half, n_rope_blocks):
    if rope_half:
        a_ref, w_ref, cos_ref, sin_ref, o_ref = refs
    else:
        a_ref, w_ref, o_ref = refs
    acc = jnp.dot(a_ref[...], w_ref[...].astype(BF16), preferred_element_type=F32)
    if not rope_half:
        o_ref[...] = acc.astype(o_ref.dtype)
        return
    j = pl.program_id(1)

    @pl.when(j < n_rope_blocks)
    def _():
        cos_t, sin_t = cos_ref[...], sin_ref[...]
        for c in range(acc.shape[1] // LANES):
            sl = slice(c * LANES, (c + 1) * LANES)
            o_ref[:, sl] = _rope_lanes(acc[:, sl], cos_t, sin_t, rope_half).astype(o_ref.dtype)

    @pl.when(j >= n_rope_blocks)
    def _():
        o_ref[...] = acc.astype(o_ref.dtype)


def matmul(a, w, *, out_dtype, rope=None, n_rope_cols=0):
    m, k = a.shape
    n = w.shape[1]
    bm = _pick(m, (1088, 1024, 512, 256, 128, 64, 32, 16, 8))
    bn = _pick(n, (256, 128)) if n % LANES == 0 else n
    in_specs = [pl.BlockSpec((bm, k), lambda i, j: (i, 0)), pl.BlockSpec((k, bn), lambda i, j: (0, j))]
    args = [a, w]
    rope_half = 0
    if rope is not None:
        cos_t, sin_t, rope_half = rope
        in_specs += [pl.BlockSpec((bm, LANES), lambda i, j: (i, 0))] * 2
        args += [cos_t, sin_t]
        assert n_rope_cols % bn == 0
    return pl.pallas_call(
        functools.partial(_matmul_kernel, rope_half=rope_half, n_rope_blocks=n_rope_cols // bn),
        grid=(m // bm, n // bn),
        in_specs=in_specs,
        out_specs=pl.BlockSpec((bm, bn), lambda i, j: (i, j)),
        out_shape=jax.ShapeDtypeStruct((m, n), out_dtype),
        compiler_params=_params("parallel", "arbitrary"),
        name="matmul",
    )(*args)


def _ffn_hidden_kernel(te_ref, act_ref, a_ref, wg_ref, wu_ref, o_ref):
    m = pl.program_id(1)

    @pl.when(act_ref[m] != 0)
    def _():
        a = a_ref[...]
        g = jnp.dot(a, wg_ref[0].astype(BF16), preferred_element_type=F32)
        u = jnp.dot(a, wu_ref[0].astype(BF16), preferred_element_type=F32)
        o_ref[...] = (g * jax.nn.sigmoid(g) * u).astype(o_ref.dtype)

    @pl.when(act_ref[m] == 0)
    def _():
        o_ref[...] = jnp.zeros_like(o_ref)


def _ffn_hidden(a, w_gate, w_up, tile_expert, tile_active, tm, bn_max):
    p, k = a.shape
    f = w_gate.shape[2]
    bn = _pick(f, tuple(c for c in (512, 256, 128) if c <= bn_max))
    w_spec = pl.BlockSpec((1, k, bn), lambda j, i, te, act: (te[i], 0, j))
    return pl.pallas_call(
        _ffn_hidden_kernel,
        grid_spec=pltpu.PrefetchScalarGridSpec(
            num_scalar_prefetch=2,
            grid=(f // bn, p // tm),
            in_specs=[pl.BlockSpec((tm, k), lambda j, i, te, act: (i, 0)), w_spec, w_spec],
            out_specs=pl.BlockSpec((tm, bn), lambda j, i, te, act: (i, j)),
        ),
        out_shape=jax.ShapeDtypeStruct((p, f), BF16),
        compiler_params=_params("parallel", "arbitrary"),
        name="ffn_hidden",
    )(tile_expert, tile_active, a, w_gate, w_up)


def _ffn_down_kernel(te_ref, act_ref, a_ref, w_ref, o_ref):
    m, kk = pl.program_id(0), pl.program_id(2)

    @pl.when(kk == 0)
    def _():
        o_ref[...] = jnp.zeros_like(o_ref)

    @pl.when(act_ref[m] != 0)
    def _():
        o_ref[...] += jnp.dot(a_ref[...], w_ref[0].astype(BF16), preferred_element_type=F32)


def _ffn_down(h, w_down, tile_expert, tile_active, tm):
    p, f = h.shape
    d = w_down.shape[2]
    bn = _pick(d, (2048, 1024, 512, 256, 128))
    bk = _pick(f, (1024, 512, 256, 128))
    return pl.pallas_call(
        _ffn_down_kernel,
        grid_spec=pltpu.PrefetchScalarGridSpec(
            num_scalar_prefetch=2,
            grid=(p // tm, d // bn, f // bk),
            in_specs=[pl.BlockSpec((tm, bk), lambda i, j, kk, te, act: (i, kk)),
                      pl.BlockSpec((1, bk, bn), lambda i, j, kk, te, act: (te[i], kk, j))],
            out_specs=pl.BlockSpec((tm, bn), lambda i, j, kk, te, act: (i, j)),
        ),
        out_shape=jax.ShapeDtypeStruct((p, d), F32),
        compiler_params=_params("parallel", "parallel", "arbitrary"),
        name="ffn_down",
    )(tile_expert, tile_active, h, w_down)


def _dense_tiles(m):
    tm = _pick(m, (1088, 1024, 512, 256, 128, 64, 32, 16, 8))
    return tm, jnp.zeros((m // tm,), I32), jnp.ones((m // tm,), I32)


def ffn_dense_hidden(a, w_gate, w_up):
    tm, te, act = _dense_tiles(a.shape[0])
    return _ffn_hidden(a, w_gate[None], w_up[None], te, act, tm, DENSE_HIDDEN_BN)


def ffn_dense_down(h, w_down):
    tm, te, act = _dense_tiles(h.shape[0])
    return _ffn_down(h, w_down[None], te, act, tm)


def moe_hidden(a, w_gate, w_up, tile_expert, tile_active):
    return _ffn_hidden(a, w_gate, w_up, tile_expert, tile_active, a.shape[0] // tile_expert.shape[0], MOE_HIDDEN_BN)


def moe_down(h, w_down, tile_expert, tile_active):
    return _ffn_down(h, w_down, tile_expert, tile_active, h.shape[0] // tile_expert.shape[0])


def _sink_column(sinks_ref, kv, rows_per_head):
    r = lax.broadcasted_iota(I32, (GROUP * rows_per_head, 1), 0)
    col = jnp.full(r.shape, sinks_ref[kv * GROUP + GROUP - 1], F32)
    for g in range(GROUP - 2, -1, -1):
        col = jnp.where(r < (g + 1) * rows_per_head, sinks_ref[kv * GROUP + g], col)
    return col


def _swa_prompt_kernel(sinks_ref, q_ref, kc_ref, kp_ref, vc_ref, vp_ref, o_ref):
    j = pl.program_id(1)
    blk = q_ref.shape[0]
    rows = GROUP * blk
    tq = lax.broadcasted_iota(I32, (rows, 2 * blk), 0) % blk + blk
    ts = lax.broadcasted_iota(I32, (rows, 2 * blk), 1)
    first_key = jnp.where(j > 0, 0, blk)
    valid = (tq - ts >= 0) & (tq - ts <= WINDOW) & (ts >= first_key)
    for kv in range(N_KV_HEADS):
        ksl = slice(kv * HEAD_DIM, (kv + 1) * HEAD_DIM)
        kk = jnp.concatenate([kp_ref[:, ksl], kc_ref[:, ksl]], axis=0).astype(BF16)
        vv = jnp.concatenate([vp_ref[:, ksl], vc_ref[:, ksl]], axis=0).astype(BF16)
        q = jnp.concatenate(
            [q_ref[:, (kv * GROUP + g) * HEAD_DIM:(kv * GROUP + g + 1) * HEAD_DIM] for g in range(GROUP)],
            axis=0).astype(BF16)
        s = jnp.where(valid, _dot_nt(q, kk) * SWA_SCALE, -jnp.inf)
        sink = _sink_column(sinks_ref, kv, blk)
        mx = jnp.maximum(jnp.max(s, axis=-1, keepdims=True), sink)
        e = jnp.exp(s - mx)
        p = e / (jnp.sum(e, axis=-1, keepdims=True) + jnp.exp(sink - mx))
        o = jnp.dot(p.astype(BF16), vv, preferred_element_type=F32)
        for g in range(GROUP):
            hsl = slice((kv * GROUP + g) * HEAD_DIM, (kv * GROUP + g + 1) * HEAD_DIM)
            o_ref[:, hsl] = o[g * blk:(g + 1) * blk].astype(o_ref.dtype)


def swa_prompt_attention(qkv, sinks, batch, seq):
    blk = WINDOW
    nb = seq // blk
    dq = N_KV_HEADS * GROUP * HEAD_DIM
    dkv = N_KV_HEADS * HEAD_DIM
    kcol, vcol = dq // dkv, dq // dkv + 1
    cur = lambda b, j: b * nb + j
    prev = lambda b, j: b * nb + jnp.maximum(j - 1, 0)
    return pl.pallas_call(
        _swa_prompt_kernel,
        grid=(batch, nb),
        in_specs=[pl.BlockSpec(memory_space=pltpu.SMEM),
                  pl.BlockSpec((blk, dq), lambda b, j: (cur(b, j), 0)),
                  pl.BlockSpec((blk, dkv), lambda b, j: (cur(b, j), kcol)),
                  pl.BlockSpec((blk, dkv), lambda b, j: (prev(b, j), kcol)),
                  pl.BlockSpec((blk, dkv), lambda b, j: (cur(b, j), vcol)),
                  pl.BlockSpec((blk, dkv), lambda b, j: (prev(b, j), vcol))],
        out_specs=pl.BlockSpec((blk, dq), lambda b, j: (cur(b, j), 0)),
        out_shape=jax.ShapeDtypeStruct((batch * seq, dq), BF16),
        compiler_params=_params("parallel", "arbitrary"),
        name="swa_prompt",
    )(sinks, qkv, qkv, qkv, qkv, qkv)


def _swa_sample_kernel(sinks_ref, q_ref, kn_ref, vn_ref, kc_ref, vc_ref, o_ref):
    n_seq, n_new = kn_ref.shape[0], kn_ref.shape[1]
    rows = GROUP * n_new
    t = lax.broadcasted_iota(I32, (rows, WINDOW), 0) % n_new
    jc = lax.broadcasted_iota(I32, (rows, WINDOW), 1)
    valid_c = jc >= t
    t1 = lax.broadcasted_iota(I32, (rows, 1), 0) % n_new

    def one_seq(s, carry):
        for kv in range(N_KV_HEADS):
            ksl = slice(kv * HEAD_DIM, (kv + 1) * HEAD_DIM)
            q = q_ref[s, kv * rows:(kv + 1) * rows, :]
            qb = q.astype(BF16)
            s_c = jnp.where(valid_c, _dot_nt(qb, kc_ref[s, :, ksl].astype(BF16)) * SWA_SCALE, -jnp.inf)
            kn = kn_ref[s, :, ksl]
            vn = vn_ref[s, :, ksl]
            s_n = [jnp.where(t1 >= i, jnp.sum(q * kn[i:i + 1, :], axis=-1, keepdims=True) * SWA_SCALE, -jnp.inf)
                   for i in range(n_new)]
            sink = _sink_column(sinks_ref, kv, n_new)
            mx = jnp.maximum(jnp.max(s_c, axis=-1, keepdims=True), sink)
            for sn in s_n:
                mx = jnp.maximum(mx, sn)
            e_c = jnp.exp(s_c - mx)
            e_n = [jnp.exp(sn - mx) for sn in s_n]
            den = jnp.sum(e_c, axis=-1, keepdims=True) + jnp.exp(sink - mx)
            for en in e_n:
                den = den + en
            o = jnp.dot((e_c / den).astype(BF16), vc_ref[s, :, ksl].astype(BF16), preferred_element_type=F32)
            for i in range(n_new):
                o = o + (e_n[i] / den) * vn[i:i + 1, :]
            o_ref[s, kv * rows:(kv + 1) * rows, :] = o.astype(o_ref.dtype)
        return carry

    lax.fori_loop(0, n_seq, one_seq, 0)


def swa_sample_attention(q, k_new, v_new, cache_k, cache_v, sinks):
    n, t, dkv = k_new.shape
    sb = _pick(n, (8, 4, 2, 1))
    rows = q.shape[1]
    return pl.pallas_call(
        _swa_sample_kernel,
        grid=(n // sb,),
        in_specs=[pl.BlockSpec(memory_space=pltpu.SMEM),
                  pl.BlockSpec((sb, rows, HEAD_DIM), lambda i: (i, 0, 0)),
                  pl.BlockSpec((sb, t, dkv), lambda i: (i, 0, 0)),
                  pl.BlockSpec((sb, t, dkv), lambda i: (i, 0, 0)),
                  pl.BlockSpec((sb, WINDOW, dkv), lambda i: (i, 0, 0)),
                  pl.BlockSpec((sb, WINDOW, dkv), lambda i: (i, 0, 0))],
        out_specs=pl.BlockSpec((sb, rows, HEAD_DIM), lambda i: (i, 0, 0)),
        out_shape=jax.ShapeDtypeStruct((n, rows, HEAD_DIM), BF16),
        compiler_params=_params("parallel"),
        name="swa_sample",
    )(sinks, q, k_new, v_new, cache_k, cache_v)


def _mla_kv_post_kernel(kv_ref, g_ref, cos_ref, sin_ref, c_ref, pe_ref, cb_ref, peb_ref):
    x = kv_ref[...]
    c = _rms(x[:, :KV_LORA], g_ref[...])
    half = QK_ROPE // 2
    x1, x2 = x[:, KV_LORA:KV_LORA + half], x[:, KV_LORA + half:]
    cos, sin = cos_ref[...], sin_ref[...]
    pe = jnp.concatenate([x1 * cos - x2 * sin, x2 * cos + x1 * sin], axis=-1)
    c_ref[...] = c
    pe_ref[...] = pe
    cb_ref[...] = c.astype(BF16)
    peb_ref[...] = pe.astype(BF16)


def mla_kv_post(kv, g_kv, cos, sin):
    t = kv.shape[0]
    bt = _pick(t, (512, 256, 128, 64, 32, 16, 8))
    half = QK_ROPE // 2
    row = lambda w: pl.BlockSpec((bt, w), lambda i: (i, 0))
    return pl.pallas_call(
        _mla_kv_post_kernel,
        grid=(t // bt,),
        in_specs=[row(KV_LORA + QK_ROPE), pl.BlockSpec((1, KV_LORA), lambda i: (0, 0)), row(half), row(half)],
        out_specs=[row(KV_LORA), row(QK_ROPE), row(KV_LORA), row(QK_ROPE)],
        out_shape=[jax.ShapeDtypeStruct((t, KV_LORA), F32), jax.ShapeDtypeStruct((t, QK_ROPE), F32),
                   jax.ShapeDtypeStruct((t, KV_LORA), BF16), jax.ShapeDtypeStruct((t, QK_ROPE), BF16)],
        compiler_params=_params("parallel"),
        name="mla_kv_post",
    )(kv, g_kv.reshape(1, KV_LORA), cos, sin)


def _mla_absorb_kernel(qn_ref, qp_ref, w_ref, ql_ref, qph_ref):
    h = pl.program_id(0)
    ql_ref[0] = _dot_nt(qn_ref[...], w_ref[...].astype(BF16)).astype(ql_ref.dtype)
    pair = qp_ref[...]
    qph_ref[0] = jnp.where(h % 2 == 0, pair[:, :QK_ROPE], pair[:, QK_ROPE:])


def mla_absorb(q_nope, q_pe, w_uk):
    t = q_nope.shape[0]
    bm = _pick(t, (1088, 1024, 512, 256, 128, 64, 32, 16, 8))
    return pl.pallas_call(
        _mla_absorb_kernel,
        grid=(MLA_HEADS, t // bm),
        in_specs=[pl.BlockSpec((bm, QK_NOPE), lambda h, i: (i, h)),
                  pl.BlockSpec((bm, 2 * QK_ROPE), lambda h, i: (i, h // 2)),
                  pl.BlockSpec((KV_LORA, QK_NOPE), lambda h, i: (0, h))],
        out_specs=[pl.BlockSpec((1, bm, KV_LORA), lambda h, i: (h, i, 0)),
                   pl.BlockSpec((1, bm, QK_ROPE), lambda h, i: (h, i, 0))],
        out_shape=[jax.ShapeDtypeStruct((MLA_HEADS, t, KV_LORA), BF16),
                   jax.ShapeDtypeStruct((MLA_HEADS, t, QK_ROPE), BF16)],
        compiler_params=_params("parallel", "arbitrary"),
        name="mla_absorb",
    )(q_nope, q_pe, w_uk)


def _mla_uv_kernel(o_ref, w_ref, out_ref):
    out_ref[...] = jnp.dot(o_ref[0], w_ref[...].astype(BF16), preferred_element_type=F32).astype(out_ref.dtype)


def mla_uv(o_lat, w_uv):
    t = o_lat.shape[1]
    bm = _pick(t, (1024, 512, 256, 128, 64, 32, 16, 8))
    return pl.pallas_call(
        _mla_uv_kernel,
        grid=(MLA_HEADS, t // bm),
        in_specs=[pl.BlockSpec((1, bm, KV_LORA), lambda h, i: (h, i, 0)),
                  pl.BlockSpec((KV_LORA, V_HEAD), lambda h, i: (0, h))],
        out_specs=pl.BlockSpec((bm, V_HEAD), lambda h, i: (i, h)),
        out_shape=jax.ShapeDtypeStruct((t, MLA_HEADS * V_HEAD), BF16),
        compiler_params=_params("parallel", "arbitrary"),
        name="mla_uv",
    )(o_lat, w_uv)


def _mla_prompt_kernel(qn_ref, qp_ref, c_ref, pe_ref, wuk_ref, wuv_ref, o_ref, kn_sc, v_sc, *, tq):
    h = pl.program_id(1)
    seq = c_ref.shape[0]
    c = c_ref[...]
    kn_sc[...] = jnp.dot(c, wuk_ref[...].astype(BF16), preferred_element_type=F32).astype(BF16)
    v_sc[...] = jnp.dot(c, wuv_ref[...].astype(BF16), preferred_element_type=F32).astype(BF16)
    for qb in range(seq // tq):
        rows = slice(qb * tq, (qb + 1) * tq)
        n_keys = (qb + 1) * tq
        pair = qp_ref[rows, :]
        qp = jnp.where(h % 2 == 0, pair[:, :QK_ROPE], pair[:, QK_ROPE:])
        s = (_dot_nt(qn_ref[rows, :], kn_sc[:n_keys, :]) + _dot_nt(qp, pe_ref[:n_keys, :])) * MLA_SCALE
        row = lax.broadcasted_iota(I32, s.shape, 0) + qb * tq
        col = lax.broadcasted_iota(I32, s.shape, 1)
        s = jnp.where(col <= row, s, -jnp.inf)
        p = jnp.exp(s - jnp.max(s, axis=-1, keepdims=True))
        o = jnp.dot(p.astype(BF16), v_sc[:n_keys, :], preferred_element_type=F32)
        o_ref[rows, :] = (o / jnp.sum(p, axis=-1, keepdims=True)).astype(o_ref.dtype)


def mla_prompt_attention(q_nope, q_pe, c, k_pe, w_uk, w_uv, batch, seq):
    tq = _pick(seq, (256, 128, 64, 32, 16))
    head_cols = lambda w: pl.BlockSpec((seq, w), lambda b, h: (b, h))
    head_w = lambda w: pl.BlockSpec((KV_LORA, w), lambda b, h: (0, h))
    return pl.pallas_call(
        functools.partial(_mla_prompt_kernel, tq=tq),
        grid=(batch, MLA_HEADS),
        in_specs=[head_cols(QK_NOPE),
                  pl.BlockSpec((seq, 2 * QK_ROPE), lambda b, h: (b, h // 2)),
                  pl.BlockSpec((seq, KV_LORA), lambda b, h: (b, 0)),
                  pl.BlockSpec((seq, QK_ROPE), lambda b, h: (b, 0)),
                  head_w(QK_NOPE), head_w(V_HEAD)],
        out_specs=head_cols(V_HEAD),
        out_shape=jax.ShapeDtypeStruct((batch * seq, MLA_HEADS * V_HEAD), BF16),
        scratch_shapes=[pltpu.VMEM((seq, QK_NOPE), BF16), pltpu.VMEM((seq, V_HEAD), BF16)],
        compiler_params=_params("parallel", "arbitrary"),
        name="mla_prompt",
    )(q_nope, q_pe, c, k_pe, w_uk, w_uv)


def _mla_sample_kernel(pt_ref, ql_ref, qp_ref, *refs, n_pages_step):
    ckv_refs = refs[:n_pages_step]
    kpet_refs = refs[n_pages_step:2 * n_pages_step]
    cn_ref, pn_ref, o_ref, m_sc, l_sc, acc_sc, s_a, s_b, key_a, key_b = refs[2 * n_pages_step:]
    g = pl.program_id(1)
    n_new = cn_ref.shape[1]
    page = ckv_refs[0].shape[1]

    @pl.when(g == 0)
    def _():
        m_sc[...] = jnp.full_like(m_sc, RUNNING_MAX_INIT)
        l_sc[...] = jnp.zeros_like(l_sc)
        acc_sc[...] = jnp.zeros_like(acc_sc)
        s_b[...] = jnp.full_like(s_b, -jnp.inf)
        key_b[...] = jnp.zeros_like(key_b)

    q, qp = ql_ref[0], qp_ref[0]

    def step(s_new, key_new, s_old, key_old):
        m_old = m_sc[...]
        m_fold = jnp.maximum(m_old, jnp.max(s_old[...], axis=-1, keepdims=True))
        alpha = jnp.exp(m_old - m_fold)
        l = alpha * l_sc[...]
        acc = alpha * acc_sc[...]
        for i in range(n_pages_step):
            sl = slice(i * page, (i + 1) * page)
            kc = ckv_refs[i][0].astype(BF16)
            key_new[sl, :] = kc
            s_new[:, sl] = (_dot_nt(q, kc) + jnp.dot(qp, kpet_refs[i][0].astype(BF16),
                                                     preferred_element_type=F32)) * MLA_SCALE
            p = jnp.exp(s_old[:, sl] - m_fold)
            l = l + jnp.sum(p, axis=-1, keepdims=True)
            acc = acc + jnp.dot(p.astype(BF16), key_old[sl, :], preferred_element_type=F32)
        m_sc[...] = m_fold
        l_sc[...] = l
        acc_sc[...] = acc

    @pl.when(g % 2 == 0)
    def _():
        step(s_a, key_a, s_b, key_b)

    @pl.when(g % 2 == 1)
    def _():
        step(s_b, key_b, s_a, key_a)

    @pl.when(g == pl.num_programs(1) - 1)
    def _():
        qf, qpf = q.astype(F32), qp.astype(F32)
        cn, pn = cn_ref[0], pn_ref[0]
        t = lax.broadcasted_iota(I32, (q.shape[0], 1), 0) % n_new
        s_n = [jnp.where(t >= i,
                         (jnp.sum(qf * cn[i:i + 1, :], axis=-1, keepdims=True)
                          + jnp.sum(qpf * pn[i:i + 1, :], axis=-1, keepdims=True)) * MLA_SCALE,
                         -jnp.inf) for i in range(n_new)]
        m_old = m_sc[...]
        m_fin = m_old
        for s in s_n:
            m_fin = jnp.maximum(m_fin, s)
        alpha = jnp.exp(m_old - m_fin)
        l = alpha * l_sc[...]
        acc = alpha * acc_sc[...]
        for i, s in enumerate(s_n):
            p = jnp.exp(s - m_fin)
            l = l + p
            acc = acc + p * cn[i:i + 1, :]
        o_ref[0] = (acc / l).astype(o_ref.dtype)


def mla_sample_attention(q_lat, q_pe, cache_ckv, cache_kpe_t, page_table, c_new, pe_new):
    n, rows, _ = q_lat.shape
    n_pages = page_table.shape[1]
    page = cache_ckv.shape[1]
    t_new = c_new.shape[1]
    gp = _pick(n_pages, (PAGES_PER_STEP, 4, 2, 1))
    n_groups = n_pages // gp

    def page_spec(shape, i):
        return pl.BlockSpec((1,) + shape,
                            lambda s, g, pt: (pt[s * n_pages + jnp.minimum(g, n_groups - 1) * gp + i], 0, 0))

    seq_spec = lambda r, w: pl.BlockSpec((1, r, w), lambda s, g, pt: (s, 0, 0))
    scores = pltpu.VMEM((rows, gp * page), F32)
    keys = pltpu.VMEM((gp * page, KV_LORA), BF16)
    return pl.pallas_call(
        functools.partial(_mla_sample_kernel, n_pages_step=gp),
        grid_spec=pltpu.PrefetchScalarGridSpec(
            num_scalar_prefetch=1,
            grid=(n, n_groups + 1),
            in_specs=([seq_spec(rows, KV_LORA), seq_spec(rows, QK_ROPE)]
                      + [page_spec((page, KV_LORA), i) for i in range(gp)]
                      + [page_spec((QK_ROPE, page), i) for i in range(gp)]
                      + [seq_spec(t_new, KV_LORA), seq_spec(t_new, QK_ROPE)]),
            out_specs=seq_spec(rows, KV_LORA),
            scratch_shapes=[pltpu.VMEM((rows, 1), F32), pltpu.VMEM((rows, 1), F32), pltpu.VMEM((rows, KV_LORA), F32),
                            scores, scores, keys, keys],
        ),
        out_shape=jax.ShapeDtypeStruct((n, rows, KV_LORA), BF16),
        compiler_params=_params("parallel", "arbitrary"),
        name="mla_sample",
    )(page_table.reshape(-1), q_lat, q_pe, *([cache_ckv] * gp), *([cache_kpe_t] * gp), c_new, pe_new)


def _router_kernel(x_ref, y_ref, g_ref, wt_ref, b_ref, xo_ref, h_ref, idx_ref, gate_ref):
    x = x_ref[...] + y_ref[...]
    xo_ref[...] = x
    h = _rms(x, g_ref[...])
    h_ref[...] = h
    logits = lax.dot_general(wt_ref[...], h, (((1,), (1,)), ((), ())), precision=lax.Precision.HIGHEST,
                             preferred_element_type=F32) + b_ref[...]
    e = lax.broadcasted_iota(I32, logits.shape, 0)
    v1 = jnp.max(logits, axis=0, keepdims=True)
    i1 = jnp.min(jnp.where(logits == v1, e, N_EXPERTS), axis=0, keepdims=True)
    rest = jnp.where(e == i1, -jnp.inf, logits)
    v2 = jnp.max(rest, axis=0, keepdims=True)
    i2 = jnp.min(jnp.where(rest == v2, e, N_EXPERTS), axis=0, keepdims=True)
    ex = jnp.exp(v2 - v1)
    idx_ref[...] = jnp.concatenate([i1, i2], axis=0)
    gate_ref[...] = jnp.concatenate([1.0 / (1.0 + ex), ex / (1.0 + ex)], axis=0)


def router(x, y, g, w_router, b_router):
    t, d = x.shape
    bt = _pick(t, (128,))
    row = pl.BlockSpec((bt, d), lambda i: (i, 0))
    top = pl.BlockSpec((TOP_K, bt), lambda i: (0, i))
    return pl.pallas_call(
        _router_kernel,
        grid=(t // bt,),
        in_specs=[row, row, pl.BlockSpec((1, d), lambda i: (0, 0)),
                  pl.BlockSpec((N_EXPERTS, d), lambda i: (0, 0)), pl.BlockSpec((N_EXPERTS, 1), lambda i: (0, 0))],
        out_specs=[row, row, top, top],
        out_shape=[jax.ShapeDtypeStruct((t, d), F32), jax.ShapeDtypeStruct((t, d), F32),
                   jax.ShapeDtypeStruct((TOP_K, t), I32), jax.ShapeDtypeStruct((TOP_K, t), F32)],
        compiler_params=_params("parallel"),
        name="router",
    )(x, y, g.reshape(1, d), w_router.T, b_router.reshape(N_EXPERTS, 1))


def _row_copy(src_ref, dst_ref, src_row, dst_row, sem):
    return pltpu.make_async_copy(src_ref.at[pl.ds(src_row, 1), :], dst_ref.at[pl.ds(dst_row, 1), :], sem)


def _gather_kernel(idx_ref, n_ref, src_ref, o_ref, buf, sem):
    rows = buf.shape[0]
    base = pl.program_id(0) * rows

    def start(r, c):
        _row_copy(src_ref, buf, idx_ref[base + r], r, sem).start()
        return c

    def wait(r, c):
        _row_copy(src_ref, buf, 0, r, sem).wait()
        return c

    @pl.when(base < n_ref[0])
    def _():
        lax.fori_loop(0, rows, start, 0, unroll=DMA_LOOP_UNROLL)
        lax.fori_loop(0, rows, wait, 0, unroll=DMA_LOOP_UNROLL)
        o_ref[...] = buf[...].astype(o_ref.dtype)

    @pl.when(base >= n_ref[0])
    def _():
        o_ref[...] = jnp.zeros_like(o_ref)


def gather_rows_bf16(src, idx, n_used):
    p = idx.shape[0]
    d = src.shape[1]
    rows = _pick(p, (GATHER_ROWS, 128, 64, 32, 16))
    return pl.pallas_call(
        _gather_kernel,
        grid_spec=pltpu.PrefetchScalarGridSpec(
            num_scalar_prefetch=2,
            grid=(p // rows,),
            in_specs=[pl.BlockSpec(memory_space=pl.ANY)],
            out_specs=pl.BlockSpec((rows, d), lambda i, idx, n: (i, 0)),
            scratch_shapes=[pltpu.VMEM((rows, d), F32), pltpu.SemaphoreType.DMA(())],
        ),
        out_shape=jax.ShapeDtypeStruct((p, d), BF16),
        compiler_params=_params("arbitrary"),
        name="moe_dispatch",
    )(idx, n_used, src)


def _combine_kernel(pos_ref, x_ref, gate_ref, g_ref, y_ref, oa_ref, ob_ref, buf, sem, *, n_a):
    rows = x_ref.shape[0]
    step = pl.program_id(0)
    base = step * rows

    def start(r, c):
        for k in range(TOP_K):
            _row_copy(y_ref, buf, pos_ref[TOP_K * (base + r) + k], k * rows + r, sem).start()
        return c

    def wait(r, c):
        _row_copy(y_ref, buf, 0, r, sem).wait()
        return c

    lax.fori_loop(0, rows, start, 0, unroll=DMA_LOOP_UNROLL)
    lax.fori_loop(0, TOP_K * rows, wait, 0, unroll=DMA_LOOP_UNROLL)
    gates = gate_ref[...]
    x = x_ref[...]
    for k in range(TOP_K):
        x = x + gates[:, k:k + 1] * buf[k * rows:(k + 1) * rows, :]
    out = _rms(x, g_ref[...])

    @pl.when(step < n_a)
    def _():
        oa_ref[...] = out

    @pl.when(step >= n_a)
    def _():
        ob_ref[...] = out


def moe_combine_norm(x, y, pos, gates, g_final, t_a):
    t, d = x.shape
    assert 0 < t_a < t
    rows = _pick(math.gcd(t_a, t - t_a), (COMBINE_ROWS, 64, 32, 16, 8))
    n_a = t_a // rows
    return pl.pallas_call(
        functools.partial(_combine_kernel, n_a=n_a),
        grid_spec=pltpu.PrefetchScalarGridSpec(
            num_scalar_prefetch=1,
            grid=(t // rows,),
            in_specs=[pl.BlockSpec((rows, d), lambda i, pos: (i, 0)),
                      pl.BlockSpec((rows, TOP_K), lambda i, pos: (i, 0)),
                      pl.BlockSpec((1, d), lambda i, pos: (0, 0)),
                      pl.BlockSpec(memory_space=pl.ANY)],
            out_specs=[pl.BlockSpec((rows, d), lambda i, pos: (jnp.minimum(i, n_a - 1), 0)),
                       pl.BlockSpec((rows, d), lambda i, pos: (jnp.maximum(i - n_a, 0), 0))],
            scratch_shapes=[pltpu.VMEM((TOP_K * rows, d), F32), pltpu.SemaphoreType.DMA(())],
        ),
        out_shape=[jax.ShapeDtypeStruct((t_a, d), F32), jax.ShapeDtypeStruct((t - t_a, d), F32)],
        compiler_params=_params("arbitrary"),
        name="moe_combine",
    )(pos, x, gates, g_final.reshape(1, d), y)


def _routing_plan(top_idx, pad, tiles):
    t = top_idx.shape[1]
    n_pairs = t * TOP_K
    n_rows = (n_pairs // pad + N_EXPERTS) * pad
    pair_expert = top_idx.T.reshape(-1)
    order = jnp.argsort(pair_expert, stable=True).astype(I32)
    sorted_expert = pair_expert[order]
    counts = jnp.sum((pair_expert[None, :] == jnp.arange(N_EXPERTS, dtype=I32)[:, None]).astype(I32), axis=1)
    padded = (counts + pad - 1) // pad * pad
    group_end = jnp.cumsum(counts)
    padded_end = jnp.cumsum(padded)
    padded_start = padded_end - padded
    rank = jnp.arange(n_pairs, dtype=I32) - (group_end - counts)[sorted_expert]
    dest = (padded_start[sorted_expert] + rank).astype(I32)
    row_token = jnp.zeros((n_rows,), I32).at[dest].set(order // TOP_K)
    pair_row = jnp.zeros((n_pairs,), I32).at[order].set(dest)
    real_end = padded_start + counts
    last_expert = jnp.sum((padded_end[-1] - 1 >= padded_end).astype(I32))
    tile_maps = []
    for tm in tiles:
        start = jnp.arange(n_rows // tm, dtype=I32) * tm
        expert = jnp.sum((start[:, None] >= padded_end[None, :]).astype(I32), axis=1)
        in_use = expert < N_EXPERTS
        expert = jnp.where(in_use, expert, last_expert)
        active = in_use & (start < real_end[expert])
        tile_maps.append((expert.astype(I32), active.astype(I32)))
    return row_token, pair_row, padded_end[-1:].astype(I32), tile_maps


def kernel(x_prompt, x_sample, cache_swa_k, cache_swa_v, cache_mla_ckv, cache_mla_kpe, page_table,
           g_attn0, w_qkv_swa, sinks, w_o_swa, g_ffn0, w_ffn_gate, w_ffn_up, w_ffn_down,
           g_attn1, w_dq, g_q, w_uq, w_dkv, g_kv, w_uk, w_uv, w_o_mla,
           g_ffn1, w_router, b_router, w_exp_gate, w_exp_up, w_exp_down, g_final):
    batch, seq, d = x_prompt.shape
    n_seq, t_new, _ = x_sample.shape
    tp, ts = batch * seq, n_seq * t_new
    past_len = page_table.shape[1] * cache_mla_ckv.shape[1]
    n_heads = N_KV_HEADS * GROUP
    dq, dkv = n_heads * HEAD_DIM, N_KV_HEADS * HEAD_DIM

    x0 = jnp.concatenate([x_prompt.reshape(tp, d), x_sample.reshape(ts, d)], axis=0)
    pos = jnp.concatenate([jnp.tile(jnp.arange(seq), batch), jnp.tile(past_len + jnp.arange(t_new), n_seq)])
    _, _, cos_swa, sin_swa = _rope_tables(pos, HEAD_DIM)
    cos_pe, sin_pe, cos_mla, sin_mla = _rope_tables(pos, QK_ROPE)

    h = rmsnorm(x0, g_attn0, BF16)
    qkv = matmul(h, w_qkv_swa, out_dtype=F32, rope=(cos_swa, sin_swa, HEAD_DIM // 2), n_rope_cols=dq + dkv)
    attn_p = swa_prompt_attention(qkv, sinks, batch, seq)
    qkv_s = qkv[tp:]
    q_s = (qkv_s[:, :dq].reshape(n_seq, t_new, N_KV_HEADS, GROUP, HEAD_DIM)
           .transpose(0, 2, 3, 1, 4).reshape(n_seq, N_KV_HEADS * GROUP * t_new, HEAD_DIM))
    k_s = qkv_s[:, dq:dq + dkv].reshape(n_seq, t_new, dkv)
    v_s = qkv_s[:, dq + dkv:].reshape(n_seq, t_new, dkv)
    attn_s = swa_sample_attention(q_s, k_s, v_s, cache_swa_k.reshape(n_seq, WINDOW, dkv),
                                  cache_swa_v.reshape(n_seq, WINDOW, dkv), sinks)
    attn_s = (attn_s.reshape(n_seq, N_KV_HEADS, GROUP, t_new, HEAD_DIM)
              .transpose(0, 3, 1, 2, 4).reshape(ts, dq))
    y = matmul(jnp.concatenate([attn_p, attn_s], axis=0), w_o_swa, out_dtype=F32)

    def window_tail(col):
        tails = [qkv[(b + 1) * seq - WINDOW:(b + 1) * seq, col:col + dkv] for b in range(batch)]
        return jnp.stack(tails).reshape(batch, WINDOW, N_KV_HEADS, HEAD_DIM)

    swa_k_prompt, swa_v_prompt = window_tail(dq), window_tail(dq + dkv)
    swa_k_sample = jnp.concatenate([cache_swa_k[:, t_new:], k_s.reshape(n_seq, t_new, N_KV_HEADS, HEAD_DIM)], axis=1)
    swa_v_sample = jnp.concatenate([cache_swa_v[:, t_new:], v_s.reshape(n_seq, t_new, N_KV_HEADS, HEAD_DIM)], axis=1)

    x1, h = add_rmsnorm(x0, y, g_ffn0)
    y = ffn_dense_down(ffn_dense_hidden(h, w_ffn_gate, w_ffn_up), w_ffn_down)

    x2, h = add_rmsnorm(x1, y, g_attn1)
    cq = rmsnorm(matmul(h, w_dq, out_dtype=F32), g_q, BF16)
    c, k_pe, c_b, k_pe_b = mla_kv_post(matmul(h, w_dkv, out_dtype=F32), g_kv, cos_pe, sin_pe)
    w_uq_h = w_uq.reshape(w_uq.shape[0], MLA_HEADS, QK_NOPE + QK_ROPE)
    q_nope = matmul(cq, w_uq_h[:, :, :QK_NOPE].reshape(-1, MLA_HEADS * QK_NOPE), out_dtype=BF16)
    q_pe = matmul(cq, w_uq_h[:, :, QK_NOPE:].reshape(-1, MLA_HEADS * QK_ROPE), out_dtype=BF16,
                  rope=(cos_mla, sin_mla, QK_ROPE // 2), n_rope_cols=MLA_HEADS * QK_ROPE)
    w_uk2 = w_uk.reshape(KV_LORA, MLA_HEADS * QK_NOPE)
    w_uv2 = w_uv.reshape(KV_LORA, MLA_HEADS * V_HEAD)
    o_p = mla_prompt_attention(q_nope, q_pe, c_b, k_pe_b, w_uk2, w_uv2, batch, seq)

    q_lat, q_pe_h = mla_absorb(q_nope[tp:], q_pe[tp:], w_uk2)

    def per_seq(a):
        return a.reshape(MLA_HEADS, n_seq, t_new, -1).transpose(1, 0, 2, 3).reshape(n_seq, MLA_HEADS * t_new, -1)

    c_s = c[tp:].reshape(n_seq, t_new, KV_LORA)
    pe_s = k_pe[tp:].reshape(n_seq, t_new, QK_ROPE)
    o_lat_s = mla_sample_attention(per_seq(q_lat), per_seq(q_pe_h), cache_mla_ckv,
                                   cache_mla_kpe.transpose(0, 2, 1), page_table, c_s, pe_s)
    o_lat_s = (o_lat_s.reshape(n_seq, MLA_HEADS, t_new, KV_LORA).transpose(1, 0, 2, 3).reshape(MLA_HEADS, ts, KV_LORA))
    o = jnp.concatenate([o_p, mla_uv(o_lat_s, w_uv2)], axis=0)
    y = matmul(o, w_o_mla, out_dtype=F32)

    x3, hn, top_idx, gates = router(x2, y, g_ffn1, w_router, b_router)
    row_token, pair_row, n_rows_used, ((te_h, act_h), (te_d, act_d)) = _routing_plan(
        top_idx, MOE_DOWN_TILE, (MOE_TILE, MOE_DOWN_TILE))
    a = gather_rows_bf16(hn, row_token, n_rows_used)
    y = moe_down(moe_hidden(a, w_exp_gate, w_exp_up, te_h, act_h), w_exp_down, te_d, act_d)
    out_p, out_s = moe_combine_norm(x3, y, pair_row, gates.T, g_final, tp)

    return (out_p.reshape(batch, seq, d), out_s.reshape(n_seq, t_new, d),
            swa_k_prompt, swa_v_prompt, swa_k_sample, swa_v_sample,
            c[:tp].reshape(batch, seq, KV_LORA), k_pe[:tp].reshape(batch, seq, QK_ROPE), c_s, pe_s)
```

```python
import functools
import math

import jax
import jax.numpy as jnp
from jax import lax
from jax.experimental import pallas as pl
from jax.experimental.pallas import tpu as pltpu

F32 = jnp.float32
BF16 = jnp.bfloat16
I32 = jnp.int32

HEAD_DIM = 128
N_KV_HEADS = 8
GROUP = 4
WINDOW = 128
MLA_HEADS = 32
KV_LORA = 512
QK_NOPE = 128
QK_ROPE = 64
V_HEAD = 128
N_EXPERTS = 8
TOP_K = 2
ROPE_THETA = 10000.0
NORM_EPS = 1e-6
MLA_SCALE = (QK_NOPE + QK_ROPE) ** -0.5
RUNNING_MAX_INIT = float(jnp.finfo(jnp.float32).min)
SWA_SCALE = HEAD_DIM ** -0.5

LANES = 128
VMEM_LIMIT = 56 * 1024 * 1024
MOE_TILE = 512
MOE_DOWN_TILE = 1024
MOE_HIDDEN_BN = 512
DENSE_HIDDEN_BN = 256
PAGES_PER_STEP = 8
GATHER_ROWS = 256
COMBINE_ROWS = 128
DMA_LOOP_UNROLL = 8


def _pick(n, candidates):
    for c in candidates:
        if n % c == 0:
            return c
    return n


def _params(*sem):
    return pltpu.CompilerParams(dimension_semantics=sem, vmem_limit_bytes=VMEM_LIMIT)


def _dot_nt(a, b):
    return lax.dot_general(a, b, (((1,), (1,)), ((), ())), preferred_element_type=F32)


def _rms(x, g):
    return x * lax.rsqrt(jnp.mean(x * x, axis=-1, keepdims=True) + NORM_EPS) * g


def _rmsnorm_kernel(x_ref, g_ref, o_ref):
    o_ref[...] = _rms(x_ref[...].astype(F32), g_ref[...]).astype(o_ref.dtype)


def rmsnorm(x, g, out_dtype):
    t, d = x.shape
    bt = _pick(t, (512, 256, 128, 64, 32, 16, 8))
    return pl.pallas_call(
        _rmsnorm_kernel,
        grid=(t // bt,),
        in_specs=[pl.BlockSpec((bt, d), lambda i: (i, 0)), pl.BlockSpec((1, d), lambda i: (0, 0))],
        out_specs=pl.BlockSpec((bt, d), lambda i: (i, 0)),
        out_shape=jax.ShapeDtypeStruct((t, d), out_dtype),
        compiler_params=_params("parallel"),
        name="rmsnorm",
    )(x, g.reshape(1, d))


def _add_rmsnorm_kernel(x_ref, y_ref, g_ref, xo_ref, h_ref):
    x = x_ref[...] + y_ref[...]
    xo_ref[...] = x
    h_ref[...] = _rms(x, g_ref[...]).astype(h_ref.dtype)


def add_rmsnorm(x, y, g):
    t, d = x.shape
    bt = _pick(t, (256, 128, 64, 32, 16, 8))
    row = pl.BlockSpec((bt, d), lambda i: (i, 0))
    return pl.pallas_call(
        _add_rmsnorm_kernel,
        grid=(t // bt,),
        in_specs=[row, row, pl.BlockSpec((1, d), lambda i: (0, 0))],
        out_specs=[row, row],
        out_shape=[jax.ShapeDtypeStruct((t, d), F32), jax.ShapeDtypeStruct((t, d), BF16)],
        compiler_params=_params("parallel"),
        name="add_rmsnorm",
    )(x, y, g.reshape(1, d))


def _rope_tables(pos, dim):
    inv = jnp.power(jnp.float32(ROPE_THETA), -jnp.arange(0, dim, 2, dtype=F32) / dim)
    ang = pos.astype(F32)[:, None] * inv[None, :]
    cos, sin = jnp.cos(ang), jnp.sin(ang)
    reps = LANES // dim
    cos_t = jnp.tile(jnp.concatenate([cos, cos], axis=-1), (1, reps))
    sin_t = jnp.tile(jnp.concatenate([-sin, sin], axis=-1), (1, reps))
    return cos, sin, cos_t, sin_t


def _rope_lanes(x, cos_t, sin_t, half):
    if 2 * half == LANES:
        rot = pltpu.roll(x, half, 1)
    else:
        lane = lax.broadcasted_iota(I32, x.shape, 1)
        first = (lane % (2 * half)) < half
        rot = jnp.where(first, pltpu.roll(x, LANES - half, 1), pltpu.roll(x, half, 1))
    return x * cos_t + rot * sin_t


def _matmul_kernel(*refs, rope_half, n_rope_blocks):
    if rope_half:
        a_ref, w_ref, cos_ref, sin_ref, o_ref = refs
    else:
        a_ref, w_ref, o_ref = refs
    acc = jnp.dot(a_ref[...], w_ref[...].astype(BF16), preferred_element_type=F32)
    if not rope_half:
        o_ref[...] = acc.astype(o_ref.dtype)
        return
    j = pl.program_id(1)

    @pl.when(j < n_rope_blocks)
    def _():
        cos_t, sin_t = cos_ref[...], sin_ref[...]
        for c in range(acc.shape[1] // LANES):
            sl = slice(c * LANES, (c + 1) * LANES)
            o_ref[:, sl] = _rope_lanes(acc[:, sl], cos_t, sin_t, rope_half).astype(o_ref.dtype)

    @pl.when(j >= n_rope_blocks)
    def _():
        o_ref[...] = acc.astype(o_ref.dtype)


def matmul(a, w, *, out_dtype, rope=None, n_rope_cols=0):
    m, k = a.shape
    n = w.shape[1]
    bm = _pick(m, (1088, 1024, 512, 256, 128, 64, 32, 16, 8))
    bn = _pick(n, (256, 128)) if n % LANES == 0 else n
    in_specs = [pl.BlockSpec((bm, k), lambda i, j: (i, 0)), pl.BlockSpec((k, bn), lambda i, j: (0, j))]
    args = [a, w]
    rope_half = 0
    if rope is not None:
        cos_t, sin_t, rope_half = rope
        in_specs += [pl.BlockSpec((bm, LANES), lambda i, j: (i, 0))] * 2
        args += [cos_t, sin_t]
        assert n_rope_cols % bn == 0
    return pl.pallas_call(
        functools.partial(_matmul_kernel, rope_half=rope_half, n_rope_blocks=n_rope_cols // bn),
        grid=(m // bm, n // bn),
        in_specs=in_specs,
        out_specs=pl.BlockSpec((bm, bn), lambda i, j: (i, j)),
        out_shape=jax.ShapeDtypeStruct((m, n), out_dtype),
        compiler_params=_params("parallel", "arbitrary"),
        name="matmul",
    )(*args)


def _last_active_tile(tile_active):
    tiles = jnp.arange(tile_active.shape[0], dtype=I32)
    return lax.cummax(jnp.where(tile_active != 0, tiles, 0), axis=0)


def _ffn_hidden_kernel(te_ref, act_ref, src_ref, a_ref, wg_ref, wu_ref, o_ref):
    m = pl.program_id(1)

    @pl.when(act_ref[m] != 0)
    def _():
        a = a_ref[...]
        g = jnp.dot(a, wg_ref[0].astype(BF16), preferred_element_type=F32)
        u = jnp.dot(a, wu_ref[0].astype(BF16), preferred_element_type=F32)
        o_ref[...] = (g * jax.nn.sigmoid(g) * u).astype(o_ref.dtype)

    @pl.when(act_ref[m] == 0)
    def _():
        o_ref[...] = jnp.zeros_like(o_ref)


def _ffn_hidden(a, w_gate, w_up, tile_expert, tile_active, tm, bn_max):
    p, k = a.shape
    f = w_gate.shape[2]
    bn = _pick(f, tuple(c for c in (512, 256, 128) if c <= bn_max))
    w_spec = pl.BlockSpec((1, k, bn), lambda j, i, te, act, src: (te[i], 0, j))
    return pl.pallas_call(
        _ffn_hidden_kernel,
        grid_spec=pltpu.PrefetchScalarGridSpec(
            num_scalar_prefetch=3,
            grid=(f // bn, p // tm),
            in_specs=[pl.BlockSpec((tm, k), lambda j, i, te, act, src: (src[i], 0)), w_spec, w_spec],
            out_specs=pl.BlockSpec((tm, bn), lambda j, i, te, act, src: (i, j)),
        ),
        out_shape=jax.ShapeDtypeStruct((p, f), BF16),
        compiler_params=_params("parallel", "arbitrary"),
        name="ffn_hidden",
    )(tile_expert, tile_active, _last_active_tile(tile_active), a, w_gate, w_up)


def _ffn_down_kernel(te_ref, act_ref, src_ref, a_ref, w_ref, o_ref):
    m, kk = pl.program_id(0), pl.program_id(2)

    @pl.when(kk == 0)
    def _():
        o_ref[...] = jnp.zeros_like(o_ref)

    @pl.when(act_ref[m] != 0)
    def _():
        o_ref[...] += jnp.dot(a_ref[...], w_ref[0].astype(BF16), preferred_element_type=F32)


def _ffn_down(h, w_down, tile_expert, tile_active, tm):
    p, f = h.shape
    d = w_down.shape[2]
    bn = _pick(d, (2048, 1024, 512, 256, 128))
    bk = _pick(f, (1024, 512, 256, 128))
    nj, nk = d // bn, f // bk

    def a_map(i, j, kk, te, act, src):
        return src[i], jnp.where(act[i] != 0, kk, nk - 1)

    def w_map(i, j, kk, te, act, src):
        return te[i], jnp.where(act[i] != 0, kk, nk - 1), jnp.where(act[i] != 0, j, nj - 1)

    return pl.pallas_call(
        _ffn_down_kernel,
        grid_spec=pltpu.PrefetchScalarGridSpec(
            num_scalar_prefetch=3,
            grid=(p // tm, nj, nk),
            in_specs=[pl.BlockSpec((tm, bk), a_map), pl.BlockSpec((1, bk, bn), w_map)],
            out_specs=pl.BlockSpec((tm, bn), lambda i, j, kk, te, act, src: (i, j)),
        ),
        out_shape=jax.ShapeDtypeStruct((p, d), F32),
        compiler_params=_params("parallel", "parallel", "arbitrary"),
        name="ffn_down",
    )(tile_expert, tile_active, _last_active_tile(tile_active), h, w_down)


def _dense_tiles(m):
    tm = _pick(m, (1088, 1024, 512, 256, 128, 64, 32, 16, 8))
    return tm, jnp.zeros((m // tm,), I32), jnp.ones((m // tm,), I32)


def ffn_dense_hidden(a, w_gate, w_up):
    tm, te, act = _dense_tiles(a.shape[0])
    return _ffn_hidden(a, w_gate[None], w_up[None], te, act, tm, DENSE_HIDDEN_BN)


def ffn_dense_down(h, w_down):
    tm, te, act = _dense_tiles(h.shape[0])
    return _ffn_down(h, w_down[None], te, act, tm)


def moe_hidden(a, w_gate, w_up, tile_expert, tile_active):
    return _ffn_hidden(a, w_gate, w_up, tile_expert, tile_active, a.shape[0] // tile_expert.shape[0], MOE_HIDDEN_BN)


def moe_down(h, w_down, tile_expert, tile_active):
    return _ffn_down(h, w_down, tile_expert, tile_active, h.shape[0] // tile_expert.shape[0])


def _sink_column(sinks_ref, kv, rows_per_head):
    r = lax.broadcasted_iota(I32, (GROUP * rows_per_head, 1), 0)
    col = jnp.full(r.shape, sinks_ref[kv * GROUP + GROUP - 1], F32)
    for g in range(GROUP - 2, -1, -1):
        col = jnp.where(r < (g + 1) * rows_per_head, sinks_ref[kv * GROUP + g], col)
    return col


def _swa_prompt_kernel(sinks_ref, q_ref, kc_ref, kp_ref, vc_ref, vp_ref, o_ref):
    j = pl.program_id(1)
    blk = q_ref.shape[0]
    rows = GROUP * blk
    tq = lax.broadcasted_iota(I32, (rows, 2 * blk), 0) % blk + blk
    ts = lax.broadcasted_iota(I32, (rows, 2 * blk), 1)
    first_key = jnp.where(j > 0, 0, blk)
    valid = (tq - ts >= 0) & (tq - ts <= WINDOW) & (ts >= first_key)
    for kv in range(N_KV_HEADS):
        ksl = slice(kv * HEAD_DIM, (kv + 1) * HEAD_DIM)
        kk = jnp.concatenate([kp_ref[:, ksl], kc_ref[:, ksl]], axis=0).astype(BF16)
        vv = jnp.concatenate([vp_ref[:, ksl], vc_ref[:, ksl]], axis=0).astype(BF16)
        q = jnp.concatenate(
            [q_ref[:, (kv * GROUP + g) * HEAD_DIM:(kv * GROUP + g + 1) * HEAD_DIM] for g in range(GROUP)],
            axis=0).astype(BF16)
        s = jnp.where(valid, _dot_nt(q, kk) * SWA_SCALE, -jnp.inf)
        sink = _sink_column(sinks_ref, kv, blk)
        mx = jnp.maximum(jnp.max(s, axis=-1, keepdims=True), sink)
        e = jnp.exp(s - mx)
        p = e / (jnp.sum(e, axis=-1, keepdims=True) + jnp.exp(sink - mx))
        o = jnp.dot(p.astype(BF16), vv, preferred_element_type=F32)
        for g in range(GROUP):
            hsl = slice((kv * GROUP + g) * HEAD_DIM, (kv * GROUP + g + 1) * HEAD_DIM)
            o_ref[:, hsl] = o[g * blk:(g + 1) * blk].astype(o_ref.dtype)


def swa_prompt_attention(qkv, sinks, batch, seq):
    blk = WINDOW
    nb = seq // blk
    dq = N_KV_HEADS * GROUP * HEAD_DIM
    dkv = N_KV_HEADS * HEAD_DIM
    kcol, vcol = dq // dkv, dq // dkv + 1
    cur = lambda b, j: b * nb + j
    prev = lambda b, j: b * nb + jnp.maximum(j - 1, 0)
    return pl.pallas_call(
        _swa_prompt_kernel,
        grid=(batch, nb),
        in_specs=[pl.BlockSpec(memory_space=pltpu.SMEM),
                  pl.BlockSpec((blk, dq), lambda b, j: (cur(b, j), 0)),
                  pl.BlockSpec((blk, dkv), lambda b, j: (cur(b, j), kcol)),
                  pl.BlockSpec((blk, dkv), lambda b, j: (prev(b, j), kcol)),
                  pl.BlockSpec((blk, dkv), lambda b, j: (cur(b, j), vcol)),
                  pl.BlockSpec((blk, dkv), lambda b, j: (prev(b, j), vcol))],
        out_specs=pl.BlockSpec((blk, dq), lambda b, j: (cur(b, j), 0)),
        out_shape=jax.ShapeDtypeStruct((batch * seq, dq), BF16),
        compiler_params=_params("parallel", "arbitrary"),
        name="swa_prompt",
    )(sinks, qkv, qkv, qkv, qkv, qkv)


def _swa_sample_kernel(sinks_ref, q_ref, kn_ref, vn_ref, kc_ref, vc_ref, o_ref):
    n_seq, n_new = kn_ref.shape[0], kn_ref.shape[1]
    rows = GROUP * n_new
    t = lax.broadcasted_iota(I32, (rows, WINDOW), 0) % n_new
    jc = lax.broadcasted_iota(I32, (rows, WINDOW), 1)
    valid_c = jc >= t
    t1 = lax.broadcasted_iota(I32, (rows, 1), 0) % n_new

    def one_seq(s, carry):
        for kv in range(N_KV_HEADS):
            ksl = slice(kv * HEAD_DIM, (kv + 1) * HEAD_DIM)
            q = q_ref[s, kv * rows:(kv + 1) * rows, :]
            qb = q.astype(BF16)
            s_c = jnp.where(valid_c, _dot_nt(qb, kc_ref[s, :, ksl].astype(BF16)) * SWA_SCALE, -jnp.inf)
            kn = kn_ref[s, :, ksl]
            vn = vn_ref[s, :, ksl]
            s_n = [jnp.where(t1 >= i, jnp.sum(q * kn[i:i + 1, :], axis=-1, keepdims=True) * SWA_SCALE, -jnp.inf)
                   for i in range(n_new)]
            sink = _sink_column(sinks_ref, kv, n_new)
            mx = jnp.maximum(jnp.max(s_c, axis=-1, keepdims=True), sink)
            for sn in s_n:
                mx = jnp.maximum(mx, sn)
            e_c = jnp.exp(s_c - mx)
            e_n = [jnp.exp(sn - mx) for sn in s_n]
            den = jnp.sum(e_c, axis=-1, keepdims=True) + jnp.exp(sink - mx)
            for en in e_n:
                den = den + en
            o = jnp.dot((e_c / den).astype(BF16), vc_ref[s, :, ksl].astype(BF16), preferred_element_type=F32)
            for i in range(n_new):
                o = o + (e_n[i] / den) * vn[i:i + 1, :]
            o_ref[s, kv * rows:(kv + 1) * rows, :] = o.astype(o_ref.dtype)
        return carry

    lax.fori_loop(0, n_seq, one_seq, 0)


def swa_sample_attention(q, k_new, v_new, cache_k, cache_v, sinks):
    n, t, dkv = k_new.shape
    sb = _pick(n, (8, 4, 2, 1))
    rows = q.shape[1]
    return pl.pallas_call(
        _swa_sample_kernel,
        grid=(n // sb,),
        in_specs=[pl.BlockSpec(memory_space=pltpu.SMEM),
                  pl.BlockSpec((sb, rows, HEAD_DIM), lambda i: (i, 0, 0)),
                  pl.BlockSpec((sb, t, dkv), lambda i: (i, 0, 0)),
                  pl.BlockSpec((sb, t, dkv), lambda i: (i, 0, 0)),
                  pl.BlockSpec((sb, WINDOW, dkv), lambda i: (i, 0, 0)),
                  pl.BlockSpec((sb, WINDOW, dkv), lambda i: (i, 0, 0))],
        out_specs=pl.BlockSpec((sb, rows, HEAD_DIM), lambda i: (i, 0, 0)),
        out_shape=jax.ShapeDtypeStruct((n, rows, HEAD_DIM), BF16),
        compiler_params=_params("parallel"),
        name="swa_sample",
    )(sinks, q, k_new, v_new, cache_k, cache_v)


def _mla_kv_post_kernel(kv_ref, g_ref, cos_ref, sin_ref, c_ref, pe_ref, cb_ref, peb_ref):
    x = kv_ref[...]
    c = _rms(x[:, :KV_LORA], g_ref[...])
    half = QK_ROPE // 2
    x1, x2 = x[:, KV_LORA:KV_LORA + half], x[:, KV_LORA + half:]
    cos, sin = cos_ref[...], sin_ref[...]
    pe = jnp.concatenate([x1 * cos - x2 * sin, x2 * cos + x1 * sin], axis=-1)
    c_ref[...] = c
    pe_ref[...] = pe
    cb_ref[...] = c.astype(BF16)
    peb_ref[...] = pe.astype(BF16)


def mla_kv_post(kv, g_kv, cos, sin):
    t = kv.shape[0]
    bt = _pick(t, (512, 256, 128, 64, 32, 16, 8))
    half = QK_ROPE // 2
    row = lambda w: pl.BlockSpec((bt, w), lambda i: (i, 0))
    return pl.pallas_call(
        _mla_kv_post_kernel,
        grid=(t // bt,),
        in_specs=[row(KV_LORA + QK_ROPE), pl.BlockSpec((1, KV_LORA), lambda i: (0, 0)), row(half), row(half)],
        out_specs=[row(KV_LORA), row(QK_ROPE), row(KV_LORA), row(QK_ROPE)],
        out_shape=[jax.ShapeDtypeStruct((t, KV_LORA), F32), jax.ShapeDtypeStruct((t, QK_ROPE), F32),
                   jax.ShapeDtypeStruct((t, KV_LORA), BF16), jax.ShapeDtypeStruct((t, QK_ROPE), BF16)],
        compiler_params=_params("parallel"),
        name="mla_kv_post",
    )(kv, g_kv.reshape(1, KV_LORA), cos, sin)


def _mla_absorb_kernel(qn_ref, qp_ref, w_ref, ql_ref, qph_ref):
    h = pl.program_id(0)
    ql_ref[0] = _dot_nt(qn_ref[...], w_ref[...].astype(BF16)).astype(ql_ref.dtype)
    pair = qp_ref[...]
    qph_ref[0] = jnp.where(h % 2 == 0, pair[:, :QK_ROPE], pair[:, QK_ROPE:])


def mla_absorb(q_nope, q_pe, w_uk):
    t = q_nope.shape[0]
    bm = _pick(t, (1088, 1024, 512, 256, 128, 64, 32, 16, 8))
    return pl.pallas_call(
        _mla_absorb_kernel,
        grid=(MLA_HEADS, t // bm),
        in_specs=[pl.BlockSpec((bm, QK_NOPE), lambda h, i: (i, h)),
                  pl.BlockSpec((bm, 2 * QK_ROPE), lambda h, i: (i, h // 2)),
                  pl.BlockSpec((KV_LORA, QK_NOPE), lambda h, i: (0, h))],
        out_specs=[pl.BlockSpec((1, bm, KV_LORA), lambda h, i: (h, i, 0)),
                   pl.BlockSpec((1, bm, QK_ROPE), lambda h, i: (h, i, 0))],
        out_shape=[jax.ShapeDtypeStruct((MLA_HEADS, t, KV_LORA), BF16),
                   jax.ShapeDtypeStruct((MLA_HEADS, t, QK_ROPE), BF16)],
        compiler_params=_params("parallel", "arbitrary"),
        name="mla_absorb",
    )(q_nope, q_pe, w_uk)


def _mla_uv_kernel(o_ref, w_ref, out_ref):
    out_ref[...] = jnp.dot(o_ref[0], w_ref[...].astype(BF16), preferred_element_type=F32).astype(out_ref.dtype)


def mla_uv(o_lat, w_uv):
    t = o_lat.shape[1]
    bm = _pick(t, (1024, 512, 256, 128, 64, 32, 16, 8))
    return pl.pallas_call(
        _mla_uv_kernel,
        grid=(MLA_HEADS, t // bm),
        in_specs=[pl.BlockSpec((1, bm, KV_LORA), lambda h, i: (h, i, 0)),
                  pl.BlockSpec((KV_LORA, V_HEAD), lambda h, i: (0, h))],
        out_specs=pl.BlockSpec((bm, V_HEAD), lambda h, i: (i, h)),
        out_shape=jax.ShapeDtypeStruct((t, MLA_HEADS * V_HEAD), BF16),
        compiler_params=_params("parallel", "arbitrary"),
        name="mla_uv",
    )(o_lat, w_uv)


def _mla_prompt_kernel(qn_ref, qp_ref, c_ref, pe_ref, wuk_ref, wuv_ref, o_ref, kn_sc, v_sc, *, tq):
    h = pl.program_id(1)
    seq = c_ref.shape[0]
    c = c_ref[...]
    kn_sc[...] = jnp.dot(c, wuk_ref[...].astype(BF16), preferred_element_type=F32).astype(BF16)
    v_sc[...] = jnp.dot(c, wuv_ref[...].astype(BF16), preferred_element_type=F32).astype(BF16)
    for qb in range(seq // tq):
        rows = slice(qb * tq, (qb + 1) * tq)
        n_keys = (qb + 1) * tq
        pair = qp_ref[rows, :]
        qp = jnp.where(h % 2 == 0, pair[:, :QK_ROPE], pair[:, QK_ROPE:])
        s = (_dot_nt(qn_ref[rows, :], kn_sc[:n_keys, :]) + _dot_nt(qp, pe_ref[:n_keys, :])) * MLA_SCALE
        row = lax.broadcasted_iota(I32, s.shape, 0) + qb * tq
        col = lax.broadcasted_iota(I32, s.shape, 1)
        s = jnp.where(col <= row, s, -jnp.inf)
        p = jnp.exp(s - jnp.max(s, axis=-1, keepdims=True))
        o = jnp.dot(p.astype(BF16), v_sc[:n_keys, :], preferred_element_type=F32)
        o_ref[rows, :] = (o / jnp.sum(p, axis=-1, keepdims=True)).astype(o_ref.dtype)


def mla_prompt_attention(q_nope, q_pe, c, k_pe, w_uk, w_uv, batch, seq):
    tq = _pick(seq, (256, 128, 64, 32, 16))
    head_cols = lambda w: pl.BlockSpec((seq, w), lambda b, h: (b, h))
    head_w = lambda w: pl.BlockSpec((KV_LORA, w), lambda b, h: (0, h))
    return pl.pallas_call(
        functools.partial(_mla_prompt_kernel, tq=tq),
        grid=(batch, MLA_HEADS),
        in_specs=[head_cols(QK_NOPE),
                  pl.BlockSpec((seq, 2 * QK_ROPE), lambda b, h: (b, h // 2)),
                  pl.BlockSpec((seq, KV_LORA), lambda b, h: (b, 0)),
                  pl.BlockSpec((seq, QK_ROPE), lambda b, h: (b, 0)),
                  head_w(QK_NOPE), head_w(V_HEAD)],
        out_specs=head_cols(V_HEAD),
        out_shape=jax.ShapeDtypeStruct((batch * seq, MLA_HEADS * V_HEAD), BF16),
        scratch_shapes=[pltpu.VMEM((seq, QK_NOPE), BF16), pltpu.VMEM((seq, V_HEAD), BF16)],
        compiler_params=_params("parallel", "arbitrary"),
        name="mla_prompt",
    )(q_nope, q_pe, c, k_pe, w_uk, w_uv)


def _mla_sample_kernel(pt_ref, ql_ref, qp_ref, *refs, n_pages_step):
    ckv_refs = refs[:n_pages_step]
    kpet_refs = refs[n_pages_step:2 * n_pages_step]
    cn_ref, pn_ref, o_ref, m_sc, l_sc, acc_sc, s_a, s_b, key_a, key_b = refs[2 * n_pages_step:]
    g = pl.program_id(1)
    n_new = cn_ref.shape[1]
    page = ckv_refs[0].shape[1]

    @pl.when(g == 0)
    def _():
        m_sc[...] = jnp.full_like(m_sc, RUNNING_MAX_INIT)
        l_sc[...] = jnp.zeros_like(l_sc)
        acc_sc[...] = jnp.zeros_like(acc_sc)
        s_b[...] = jnp.full_like(s_b, -jnp.inf)
        key_b[...] = jnp.zeros_like(key_b)

    q, qp = ql_ref[0], qp_ref[0]

    def step(s_new, key_new, s_old, key_old):
        m_old = m_sc[...]
        m_fold = jnp.maximum(m_old, jnp.max(s_old[...], axis=-1, keepdims=True))
        alpha = jnp.exp(m_old - m_fold)
        l = alpha * l_sc[...]
        acc = alpha * acc_sc[...]
        for i in range(n_pages_step):
            sl = slice(i * page, (i + 1) * page)
            kc = ckv_refs[i][0].astype(BF16)
            key_new[sl, :] = kc
            s_new[:, sl] = (_dot_nt(q, kc) + jnp.dot(qp, kpet_refs[i][0].astype(BF16),
                                                     preferred_element_type=F32)) * MLA_SCALE
            p = jnp.exp(s_old[:, sl] - m_fold)
            l = l + jnp.sum(p, axis=-1, keepdims=True)
            acc = acc + jnp.dot(p.astype(BF16), key_old[sl, :], preferred_element_type=F32)
        m_sc[...] = m_fold
        l_sc[...] = l
        acc_sc[...] = acc

    @pl.when(g % 2 == 0)
    def _():
        step(s_a, key_a, s_b, key_b)

    @pl.when(g % 2 == 1)
    def _():
        step(s_b, key_b, s_a, key_a)

    @pl.when(g == pl.num_programs(1) - 1)
    def _():
        qf, qpf = q.astype(F32), qp.astype(F32)
        cn, pn = cn_ref[0], pn_ref[0]
        t = lax.broadcasted_iota(I32, (q.shape[0], 1), 0) % n_new
        s_n = [jnp.where(t >= i,
                         (jnp.sum(qf * cn[i:i + 1, :], axis=-1, keepdims=True)
                          + jnp.sum(qpf * pn[i:i + 1, :], axis=-1, keepdims=True)) * MLA_SCALE,
                         -jnp.inf) for i in range(n_new)]
        m_old = m_sc[...]
        m_fin = m_old
        for s in s_n:
            m_fin = jnp.maximum(m_fin, s)
        alpha = jnp.exp(m_old - m_fin)
        l = alpha * l_sc[...]
        acc = alpha * acc_sc[...]
        for i, s in enumerate(s_n):
            p = jnp.exp(s - m_fin)
            l = l + p
            acc = acc + p * cn[i:i + 1, :]
        o_ref[0] = (acc / l).astype(o_ref.dtype)


def mla_sample_attention(q_lat, q_pe, cache_ckv, cache_kpe_t, page_table, c_new, pe_new):
    n, rows, _ = q_lat.shape
    n_pages = page_table.shape[1]
    page = cache_ckv.shape[1]
    t_new = c_new.shape[1]
    gp = _pick(n_pages, (PAGES_PER_STEP, 4, 2, 1))
    n_groups = n_pages // gp

    def page_spec(shape, i):
        return pl.BlockSpec((1,) + shape,
                            lambda s, g, pt: (pt[s * n_pages + jnp.minimum(g, n_groups - 1) * gp + i], 0, 0))

    seq_spec = lambda r, w: pl.BlockSpec((1, r, w), lambda s, g, pt: (s, 0, 0))
    scores = pltpu.VMEM((rows, gp * page), F32)
    keys = pltpu.VMEM((gp * page, KV_LORA), BF16)
    return pl.pallas_call(
        functools.partial(_mla_sample_kernel, n_pages_step=gp),
        grid_spec=pltpu.PrefetchScalarGridSpec(
            num_scalar_prefetch=1,
            grid=(n, n_groups + 1),
            in_specs=([seq_spec(rows, KV_LORA), seq_spec(rows, QK_ROPE)]
                      + [page_spec((page, KV_LORA), i) for i in range(gp)]
                      + [page_spec((QK_ROPE, page), i) for i in range(gp)]
                      + [seq_spec(t_new, KV_LORA), seq_spec(t_new, QK_ROPE)]),
            out_specs=seq_spec(rows, KV_LORA),
            scratch_shapes=[pltpu.VMEM((rows, 1), F32), pltpu.VMEM((rows, 1), F32), pltpu.VMEM((rows, KV_LORA), F32),
                            scores, scores, keys, keys],
        ),
        out_shape=jax.ShapeDtypeStruct((n, rows, KV_LORA), BF16),
        compiler_params=_params("parallel", "arbitrary"),
        name="mla_sample",
    )(page_table.reshape(-1), q_lat, q_pe, *([cache_ckv] * gp), *([cache_kpe_t] * gp), c_new, pe_new)


def _router_kernel(x_ref, y_ref, g_ref, wt_ref, b_ref, xo_ref, h_ref, idx_ref, gate_ref):
    x = x_ref[...] + y_ref[...]
    xo_ref[...] = x
    h = _rms(x, g_ref[...])
    h_ref[...] = h
    logits = lax.dot_general(wt_ref[...], h, (((1,), (1,)), ((), ())), precision=lax.Precision.HIGHEST,
                             preferred_element_type=F32) + b_ref[...]
    e = lax.broadcasted_iota(I32, logits.shape, 0)
    v1 = jnp.max(logits, axis=0, keepdims=True)
    i1 = jnp.min(jnp.where(logits == v1, e, N_EXPERTS), axis=0, keepdims=True)
    rest = jnp.where(e == i1, -jnp.inf, logits)
    v2 = jnp.max(rest, axis=0, keepdims=True)
    i2 = jnp.min(jnp.where(rest == v2, e, N_EXPERTS), axis=0, keepdims=True)
    ex = jnp.exp(v2 - v1)
    idx_ref[...] = jnp.concatenate([i1, i2], axis=0)
    gate_ref[...] = jnp.concatenate([1.0 / (1.0 + ex), ex / (1.0 + ex)], axis=0)


def router(x, y, g, w_router, b_router):
    t, d = x.shape
    bt = _pick(t, (128,))
    row = pl.BlockSpec((bt, d), lambda i: (i, 0))
    top = pl.BlockSpec((TOP_K, bt), lambda i: (0, i))
    return pl.pallas_call(
        _router_kernel,
        grid=(t // bt,),
        in_specs=[row, row, pl.BlockSpec((1, d), lambda i: (0, 0)),
                  pl.BlockSpec((N_EXPERTS, d), lambda i: (0, 0)), pl.BlockSpec((N_EXPERTS, 1), lambda i: (0, 0))],
        out_specs=[row, row, top, top],
        out_shape=[jax.ShapeDtypeStruct((t, d), F32), jax.ShapeDtypeStruct((t, d), F32),
                   jax.ShapeDtypeStruct((TOP_K, t), I32), jax.ShapeDtypeStruct((TOP_K, t), F32)],
        compiler_params=_params("parallel"),
        name="router",
    )(x, y, g.reshape(1, d), w_router.T, b_router.reshape(N_EXPERTS, 1))


def _row_copy(src_ref, dst_ref, src_row, dst_row, sem):
    return pltpu.make_async_copy(src_ref.at[pl.ds(src_row, 1), :], dst_ref.at[pl.ds(dst_row, 1), :], sem)


def _gather_kernel(idx_ref, act_ref, src_ref, o_ref, buf, sem):
    rows = buf.shape[0]
    step = pl.program_id(0)
    base = step * rows

    def start(r, c):
        _row_copy(src_ref, buf, idx_ref[base + r], r, sem).start()
        return c

    def wait(r, c):
        _row_copy(src_ref, buf, 0, r, sem).wait()
        return c

    @pl.when(act_ref[step] != 0)
    def _():
        lax.fori_loop(0, rows, start, 0, unroll=DMA_LOOP_UNROLL)
        lax.fori_loop(0, rows, wait, 0, unroll=DMA_LOOP_UNROLL)
        o_ref[...] = buf[...].astype(o_ref.dtype)

    @pl.when(act_ref[step] == 0)
    def _():
        o_ref[...] = jnp.zeros_like(o_ref)


def gather_rows_bf16(src, idx, step_active):
    p = idx.shape[0]
    d = src.shape[1]
    rows = p // step_active.shape[0]
    return pl.pallas_call(
        _gather_kernel,
        grid_spec=pltpu.PrefetchScalarGridSpec(
            num_scalar_prefetch=2,
            grid=(p // rows,),
            in_specs=[pl.BlockSpec(memory_space=pl.ANY)],
            out_specs=pl.BlockSpec((rows, d), lambda i, idx, n: (i, 0)),
            scratch_shapes=[pltpu.VMEM((rows, d), F32), pltpu.SemaphoreType.DMA(())],
        ),
        out_shape=jax.ShapeDtypeStruct((p, d), BF16),
        compiler_params=_params("arbitrary"),
        name="moe_dispatch",
    )(idx, step_active, src)


def _combine_kernel(pos_ref, x_ref, gate_ref, g_ref, y_ref, oa_ref, ob_ref, buf, sem, *, n_a):
    rows = x_ref.shape[0]
    step = pl.program_id(0)
    base = step * rows

    def start(r, c):
        for k in range(TOP_K):
            _row_copy(y_ref, buf, pos_ref[TOP_K * (base + r) + k], k * rows + r, sem).start()
        return c

    def wait(r, c):
        _row_copy(y_ref, buf, 0, r, sem).wait()
        return c

    lax.fori_loop(0, rows, start, 0, unroll=DMA_LOOP_UNROLL)
    lax.fori_loop(0, TOP_K * rows, wait, 0, unroll=DMA_LOOP_UNROLL)
    gates = gate_ref[...]
    x = x_ref[...]
    for k in range(TOP_K):
        x = x + gates[:, k:k + 1] * buf[k * rows:(k + 1) * rows, :]
    out = _rms(x, g_ref[...])

    @pl.when(step < n_a)
    def _():
        oa_ref[...] = out

    @pl.when(step >= n_a)
    def _():
        ob_ref[...] = out


def moe_combine_norm(x, y, pos, gates, g_final, t_a):
    t, d = x.shape
    assert 0 < t_a < t
    rows = _pick(math.gcd(t_a, t - t_a), (COMBINE_ROWS, 64, 32, 16, 8))
    n_a = t_a // rows
    return pl.pallas_call(
        functools.partial(_combine_kernel, n_a=n_a),
        grid_spec=pltpu.PrefetchScalarGridSpec(
            num_scalar_prefetch=1,
            grid=(t // rows,),
            in_specs=[pl.BlockSpec((rows, d), lambda i, pos: (i, 0)),
                      pl.BlockSpec((rows, TOP_K), lambda i, pos: (i, 0)),
                      pl.BlockSpec((1, d), lambda i, pos: (0, 0)),
                      pl.BlockSpec(memory_space=pl.ANY)],
            out_specs=[pl.BlockSpec((rows, d), lambda i, pos: (jnp.minimum(i, n_a - 1), 0)),
                       pl.BlockSpec((rows, d), lambda i, pos: (jnp.maximum(i - n_a, 0), 0))],
            scratch_shapes=[pltpu.VMEM((TOP_K * rows, d), F32), pltpu.SemaphoreType.DMA(())],
        ),
        out_shape=[jax.ShapeDtypeStruct((t_a, d), F32), jax.ShapeDtypeStruct((t - t_a, d), F32)],
        compiler_params=_params("arbitrary"),
        name="moe_combine",
    )(pos, x, gates, g_final.reshape(1, d), y)


def _routing_plan(top_idx, pad, tiles):
    t = top_idx.shape[1]
    n_pairs = t * TOP_K
    n_rows = (n_pairs // pad + N_EXPERTS) * pad
    pair_expert = top_idx.T.reshape(-1)
    order = jnp.argsort(pair_expert, stable=True).astype(I32)
    sorted_expert = pair_expert[order]
    counts = jnp.sum((pair_expert[None, :] == jnp.arange(N_EXPERTS, dtype=I32)[:, None]).astype(I32), axis=1)
    padded = (counts + pad - 1) // pad * pad
    group_end = jnp.cumsum(counts)
    padded_end = jnp.cumsum(padded)
    padded_start = padded_end - padded
    rank = jnp.arange(n_pairs, dtype=I32) - (group_end - counts)[sorted_expert]
    dest = (padded_start[sorted_expert] + rank).astype(I32)
    row_token = jnp.zeros((n_rows,), I32).at[dest].set(order // TOP_K)
    pair_row = jnp.zeros((n_pairs,), I32).at[order].set(dest)
    real_end = padded_start + counts
    last_expert = jnp.sum((padded_end[-1] - 1 >= padded_end).astype(I32))
    tile_maps = []
    for tm in tiles:
        start = jnp.arange(n_rows // tm, dtype=I32) * tm
        expert = jnp.sum((start[:, None] >= padded_end[None, :]).astype(I32), axis=1)
        in_use = expert < N_EXPERTS
        expert = jnp.where(in_use, expert, last_expert)
        active = in_use & (start < real_end[expert])
        tile_maps.append((expert.astype(I32), active.astype(I32)))
    return row_token, pair_row, tile_maps


def kernel(x_prompt, x_sample, cache_swa_k, cache_swa_v, cache_mla_ckv, cache_mla_kpe, page_table,
           g_attn0, w_qkv_swa, sinks, w_o_swa, g_ffn0, w_ffn_gate, w_ffn_up, w_ffn_down,
           g_attn1, w_dq, g_q, w_uq, w_dkv, g_kv, w_uk, w_uv, w_o_mla,
           g_ffn1, w_router, b_router, w_exp_gate, w_exp_up, w_exp_down, g_final):
    batch, seq, d = x_prompt.shape
    n_seq, t_new, _ = x_sample.shape
    tp, ts = batch * seq, n_seq * t_new
    past_len = page_table.shape[1] * cache_mla_ckv.shape[1]
    n_heads = N_KV_HEADS * GROUP
    dq, dkv = n_heads * HEAD_DIM, N_KV_HEADS * HEAD_DIM

    x0 = jnp.concatenate([x_prompt.reshape(tp, d), x_sample.reshape(ts, d)], axis=0)
    pos = jnp.concatenate([jnp.tile(jnp.arange(seq), batch), jnp.tile(past_len + jnp.arange(t_new), n_seq)])
    _, _, cos_swa, sin_swa = _rope_tables(pos, HEAD_DIM)
    cos_pe, sin_pe, cos_mla, sin_mla = _rope_tables(pos, QK_ROPE)

    h = rmsnorm(x0, g_attn0, BF16)
    qkv = matmul(h, w_qkv_swa, out_dtype=F32, rope=(cos_swa, sin_swa, HEAD_DIM // 2), n_rope_cols=dq + dkv)
    attn_p = swa_prompt_attention(qkv, sinks, batch, seq)
    qkv_s = qkv[tp:]
    q_s = (qkv_s[:, :dq].reshape(n_seq, t_new, N_KV_HEADS, GROUP, HEAD_DIM)
           .transpose(0, 2, 3, 1, 4).reshape(n_seq, N_KV_HEADS * GROUP * t_new, HEAD_DIM))
    k_s = qkv_s[:, dq:dq + dkv].reshape(n_seq, t_new, dkv)
    v_s = qkv_s[:, dq + dkv:].reshape(n_seq, t_new, dkv)
    attn_s = swa_sample_attention(q_s, k_s, v_s, cache_swa_k.reshape(n_seq, WINDOW, dkv),
                                  cache_swa_v.reshape(n_seq, WINDOW, dkv), sinks)
    attn_s = (attn_s.reshape(n_seq, N_KV_HEADS, GROUP, t_new, HEAD_DIM)
              .transpose(0, 3, 1, 2, 4).reshape(ts, dq))
    y = matmul(jnp.concatenate([attn_p, attn_s], axis=0), w_o_swa, out_dtype=F32)

    def window_tail(col):
        tails = [qkv[(b + 1) * seq - WINDOW:(b + 1) * seq, col:col + dkv] for b in range(batch)]
        return jnp.stack(tails).reshape(batch, WINDOW, N_KV_HEADS, HEAD_DIM)

    swa_k_prompt, swa_v_prompt = window_tail(dq), window_tail(dq + dkv)
    swa_k_sample = jnp.concatenate([cache_swa_k[:, t_new:], k_s.reshape(n_seq, t_new, N_KV_HEADS, HEAD_DIM)], axis=1)
    swa_v_sample = jnp.concatenate([cache_swa_v[:, t_new:], v_s.reshape(n_seq, t_new, N_KV_HEADS, HEAD_DIM)], axis=1)

    x1, h = add_rmsnorm(x0, y, g_ffn0)
    y = ffn_dense_down(ffn_dense_hidden(h, w_ffn_gate, w_ffn_up), w_ffn_down)

    x2, h = add_rmsnorm(x1, y, g_attn1)
    cq = rmsnorm(matmul(h, w_dq, out_dtype=F32), g_q, BF16)
    c, k_pe, c_b, k_pe_b = mla_kv_post(matmul(h, w_dkv, out_dtype=F32), g_kv, cos_pe, sin_pe)
    w_uq_h = w_uq.reshape(w_uq.shape[0], MLA_HEADS, QK_NOPE + QK_ROPE)
    q_nope = matmul(cq, w_uq_h[:, :, :QK_NOPE].reshape(-1, MLA_HEADS * QK_NOPE), out_dtype=BF16)
    q_pe = matmul(cq, w_uq_h[:, :, QK_NOPE:].reshape(-1, MLA_HEADS * QK_ROPE), out_dtype=BF16,
                  rope=(cos_mla, sin_mla, QK_ROPE // 2), n_rope_cols=MLA_HEADS * QK_ROPE)
    w_uk2 = w_uk.reshape(KV_LORA, MLA_HEADS * QK_NOPE)
    w_uv2 = w_uv.reshape(KV_LORA, MLA_HEADS * V_HEAD)
    o_p = mla_prompt_attention(q_nope, q_pe, c_b, k_pe_b, w_uk2, w_uv2, batch, seq)

    q_lat, q_pe_h = mla_absorb(q_nope[tp:], q_pe[tp:], w_uk2)

    def per_seq(a):
        return a.reshape(MLA_HEADS, n_seq, t_new, -1).transpose(1, 0, 2, 3).reshape(n_seq, MLA_HEADS * t_new, -1)

    c_s = c[tp:].reshape(n_seq, t_new, KV_LORA)
    pe_s = k_pe[tp:].reshape(n_seq, t_new, QK_ROPE)
    o_lat_s = mla_sample_attention(per_seq(q_lat), per_seq(q_pe_h), cache_mla_ckv,
                                   cache_mla_kpe.transpose(0, 2, 1), page_table, c_s, pe_s)
    o_lat_s = (o_lat_s.reshape(n_seq, MLA_HEADS, t_new, KV_LORA).transpose(1, 0, 2, 3).reshape(MLA_HEADS, ts, KV_LORA))
    o = jnp.concatenate([o_p, mla_uv(o_lat_s, w_uv2)], axis=0)
    y = matmul(o, w_o_mla, out_dtype=F32)

    x3, hn, top_idx, gates = router(x2, y, g_ffn1, w_router, b_router)
    row_token, pair_row, ((te_h, act_h), (te_d, act_d), (_, act_g)) = _routing_plan(
        top_idx, MOE_DOWN_TILE, (MOE_TILE, MOE_DOWN_TILE, GATHER_ROWS))
    a = gather_rows_bf16(hn, row_token, act_g)
    y = moe_down(moe_hidden(a, w_exp_gate, w_exp_up, te_h, act_h), w_exp_down, te_d, act_d)
    out_p, out_s = moe_combine_norm(x3, y, pair_row, gates.T, g_final, tp)

    return (out_p.reshape(batch, seq, d), out_s.reshape(n_seq, t_new, d),
            swa_k_prompt, swa_v_prompt, swa_k_sample, swa_v_sample,
            c[:tp].reshape(batch, seq, KV_LORA), k_pe[:tp].reshape(batch, seq, QK_ROPE), c_s, pe_s)
```

```python
import functools
import math

import jax
import jax.numpy as jnp
from jax import lax
from jax.experimental import pallas as pl
from jax.experimental.pallas import tpu as pltpu

F32 = jnp.float32
BF16 = jnp.bfloat16
I32 = jnp.int32

HEAD_DIM = 128
N_KV_HEADS = 8
GROUP = 4
WINDOW = 128
MLA_HEADS = 32
KV_LORA = 512
QK_NOPE = 128
QK_ROPE = 64
V_HEAD = 128
N_EXPERTS = 8
TOP_K = 2
ROPE_THETA = 10000.0
NORM_EPS = 1e-6
MLA_SCALE = (QK_NOPE + QK_ROPE) ** -0.5
RUNNING_MAX_INIT = float(jnp.finfo(jnp.float32).min)
SWA_SCALE = HEAD_DIM ** -0.5

LANES = 128
VMEM_LIMIT = 56 * 1024 * 1024
MOE_TILE = 512
MOE_DOWN_TILE = 1024
MOE_HIDDEN_BN = 512
DENSE_HIDDEN_BN = 256
PAGES_PER_STEP = 8
GATHER_ROWS = 256
COMBINE_ROWS = 128
DMA_LOOP_UNROLL = 8


def _pick(n, candidates):
    for c in candidates:
        if n % c == 0:
            return c
    return n


def _params(*sem):
    return pltpu.CompilerParams(dimension_semantics=sem, vmem_limit_bytes=VMEM_LIMIT)


def _dot_nt(a, b):
    return lax.dot_general(a, b, (((1,), (1,)), ((), ())), preferred_element_type=F32)


def _rms(x, g):
    return x * lax.rsqrt(jnp.mean(x * x, axis=-1, keepdims=True) + NORM_EPS) * g


def _rmsnorm_kernel(x_ref, g_ref, o_ref):
    o_ref[...] = _rms(x_ref[...].astype(F32), g_ref[...]).astype(o_ref.dtype)


def rmsnorm(x, g, out_dtype):
    t, d = x.shape
    bt = _pick(t, (512, 256, 128, 64, 32, 16, 8))
    return pl.pallas_call(
        _rmsnorm_kernel,
        grid=(t // bt,),
        in_specs=[pl.BlockSpec((bt, d), lambda i: (i, 0)), pl.BlockSpec((1, d), lambda i: (0, 0))],
        out_specs=pl.BlockSpec((bt, d), lambda i: (i, 0)),
        out_shape=jax.ShapeDtypeStruct((t, d), out_dtype),
        compiler_params=_params("parallel"),
        name="rmsnorm",
    )(x, g.reshape(1, d))


def _add_rmsnorm_kernel(x_ref, y_ref, g_ref, xo_ref, h_ref):
    x = x_ref[...] + y_ref[...]
    xo_ref[...] = x
    h_ref[...] = _rms(x, g_ref[...]).astype(h_ref.dtype)


def add_rmsnorm(x, y, g):
    t, d = x.shape
    bt = _pick(t, (256, 128, 64, 32, 16, 8))
    row = pl.BlockSpec((bt, d), lambda i: (i, 0))
    return pl.pallas_call(
        _add_rmsnorm_kernel,
        grid=(t // bt,),
        in_specs=[row, row, pl.BlockSpec((1, d), lambda i: (0, 0))],
        out_specs=[row, row],
        out_shape=[jax.ShapeDtypeStruct((t, d), F32), jax.ShapeDtypeStruct((t, d), BF16)],
        compiler_params=_params("parallel"),
        name="add_rmsnorm",
    )(x, y, g.reshape(1, d))


def _rope_tables(pos, dim):
    inv = jnp.power(jnp.float32(ROPE_THETA), -jnp.arange(0, dim, 2, dtype=F32) / dim)
    ang = pos.astype(F32)[:, None] * inv[None, :]
    cos, sin = jnp.cos(ang), jnp.sin(ang)
    reps = LANES // dim
    cos_t = jnp.tile(jnp.concatenate([cos, cos], axis=-1), (1, reps))
    sin_t = jnp.tile(jnp.concatenate([-sin, sin], axis=-1), (1, reps))
    return cos, sin, cos_t, sin_t


def _rope_lanes(x, cos_t, sin_t, half):
    if 2 * half == LANES:
        rot = pltpu.roll(x, half, 1)
    else:
        lane = lax.broadcasted_iota(I32, x.shape, 1)
        first = (lane % (2 * half)) < half
        rot = jnp.where(first, pltpu.roll(x, LANES - half, 1), pltpu.roll(x, half, 1))
    return x * cos_t + rot * sin_t


def _matmul_kernel(*refs, rope_half, n_rope_blocks):
    if rope_half:
        a_ref, w_ref, cos_ref, sin_ref, o_ref = refs
    else:
        a_ref, w_ref, o_ref = refs
    acc = jnp.dot(a_ref[...], w_ref[...].astype(BF16), preferred_element_type=F32)
    if not rope_half:
        o_ref[...] = acc.astype(o_ref.dtype)
        return
    j = pl.program_id(1)

    @pl.when(j < n_rope_blocks)
    def _():
        cos_t, sin_t = cos_ref[...], sin_ref[...]
        for c in range(acc.shape[1] // LANES):
            sl = slice(c * LANES, (c + 1) * LANES)
            o_ref[:, sl] = _rope_lanes(acc[:, sl], cos_t, sin_t, rope_half).astype(o_ref.dtype)

    @pl.when(j >= n_rope_blocks)
    def _():
        o_ref[...] = acc.astype(o_ref.dtype)


def matmul(a, w, *, out_dtype, rope=None, n_rope_cols=0):
    m, k = a.shape
    n = w.shape[1]
    bm = _pick(m, (1088, 1024, 512, 256, 128, 64, 32, 16, 8))
    bn = _pick(n, (256, 128)) if n % LANES == 0 else n
    in_specs = [pl.BlockSpec((bm, k), lambda i, j: (i, 0)), pl.BlockSpec((k, bn), lambda i, j: (0, j))]
    args = [a, w]
    rope_half = 0
    if rope is not None:
        cos_t, sin_t, rope_half = rope
        in_specs += [pl.BlockSpec((bm, LANES), lambda i, j: (i, 0))] * 2
        args += [cos_t, sin_t]
        assert n_rope_cols % bn == 0
    return pl.pallas_call(
        functools.partial(_matmul_kernel, rope_half=rope_half, n_rope_blocks=n_rope_cols // bn),
        grid=(m // bm, n // bn),
        in_specs=in_specs,
        out_specs=pl.BlockSpec((bm, bn), lambda i, j: (i, j)),
        out_shape=jax.ShapeDtypeStruct((m, n), out_dtype),
        compiler_params=_params("parallel", "arbitrary"),
        name="matmul",
    )(*args)


def _last_active_tile(tile_active):
    tiles = jnp.arange(tile_active.shape[0], dtype=I32)
    return lax.cummax(jnp.where(tile_active != 0, tiles, 0), axis=0)


def _ffn_hidden_kernel(te_ref, act_ref, src_ref, a_ref, wg_ref, wu_ref, o_ref):
    m = pl.program_id(1)

    @pl.when(act_ref[m] != 0)
    def _():
        a = a_ref[...]
        g = jnp.dot(a, wg_ref[0].astype(BF16), preferred_element_type=F32)
        u = jnp.dot(a, wu_ref[0].astype(BF16), preferred_element_type=F32)
        o_ref[...] = (g * jax.nn.sigmoid(g) * u).astype(o_ref.dtype)

    @pl.when(act_ref[m] == 0)
    def _():
        o_ref[...] = jnp.zeros_like(o_ref)


def _ffn_hidden(a, w_gate, w_up, tile_expert, tile_active, tm, bn_max):
    p, k = a.shape
    f = w_gate.shape[2]
    bn = _pick(f, tuple(c for c in (512, 256, 128) if c <= bn_max))
    w_spec = pl.BlockSpec((1, k, bn), lambda j, i, te, act, src: (te[i], 0, j))
    return pl.pallas_call(
        _ffn_hidden_kernel,
        grid_spec=pltpu.PrefetchScalarGridSpec(
            num_scalar_prefetch=3,
            grid=(f // bn, p // tm),
            in_specs=[pl.BlockSpec((tm, k), lambda j, i, te, act, src: (src[i], 0)), w_spec, w_spec],
            out_specs=pl.BlockSpec((tm, bn), lambda j, i, te, act, src: (i, j)),
        ),
        out_shape=jax.ShapeDtypeStruct((p, f), BF16),
        compiler_params=_params("parallel", "arbitrary"),
        name="ffn_hidden",
    )(tile_expert, tile_active, _last_active_tile(tile_active), a, w_gate, w_up)


def _ffn_down_kernel(te_ref, act_ref, src_ref, a_ref, w_ref, o_ref):
    m, kk = pl.program_id(0), pl.program_id(2)

    @pl.when(kk == 0)
    def _():
        o_ref[...] = jnp.zeros_like(o_ref)

    @pl.when(act_ref[m] != 0)
    def _():
        o_ref[...] += jnp.dot(a_ref[...], w_ref[0].astype(BF16), preferred_element_type=F32)


def _ffn_down(h, w_down, tile_expert, tile_active, tm):
    p, f = h.shape
    d = w_down.shape[2]
    bn = _pick(d, (2048, 1024, 512, 256, 128))
    bk = _pick(f, (1024, 512, 256, 128))
    nj, nk = d // bn, f // bk

    def a_map(i, j, kk, te, act, src):
        return src[i], jnp.where(act[i] != 0, kk, nk - 1)

    def w_map(i, j, kk, te, act, src):
        return te[i], jnp.where(act[i] != 0, kk, nk - 1), jnp.where(act[i] != 0, j, nj - 1)

    return pl.pallas_call(
        _ffn_down_kernel,
        grid_spec=pltpu.PrefetchScalarGridSpec(
            num_scalar_prefetch=3,
            grid=(p // tm, nj, nk),
            in_specs=[pl.BlockSpec((tm, bk), a_map), pl.BlockSpec((1, bk, bn), w_map)],
            out_specs=pl.BlockSpec((tm, bn), lambda i, j, kk, te, act, src: (i, j)),
        ),
        out_shape=jax.ShapeDtypeStruct((p, d), F32),
        compiler_params=_params("parallel", "parallel", "arbitrary"),
        name="ffn_down",
    )(tile_expert, tile_active, _last_active_tile(tile_active), h, w_down)


def _dense_tiles(m):
    tm = _pick(m, (1088, 1024, 512, 256, 128, 64, 32, 16, 8))
    return tm, jnp.zeros((m // tm,), I32), jnp.ones((m // tm,), I32)


def ffn_dense_hidden(a, w_gate, w_up):
    tm, te, act = _dense_tiles(a.shape[0])
    return _ffn_hidden(a, w_gate[None], w_up[None], te, act, tm, DENSE_HIDDEN_BN)


def ffn_dense_down(h, w_down):
    tm, te, act = _dense_tiles(h.shape[0])
    return _ffn_down(h, w_down[None], te, act, tm)


def moe_hidden(a, w_gate, w_up, tile_expert, tile_active):
    return _ffn_hidden(a, w_gate, w_up, tile_expert, tile_active, a.shape[0] // tile_expert.shape[0], MOE_HIDDEN_BN)


def moe_down(h, w_down, tile_expert, tile_active):
    return _ffn_down(h, w_down, tile_expert, tile_active, h.shape[0] // tile_expert.shape[0])


def _sink_column(sinks_ref, kv, rows_per_head):
    r = lax.broadcasted_iota(I32, (GROUP * rows_per_head, 1), 0)
    col = jnp.full(r.shape, sinks_ref[kv * GROUP + GROUP - 1], F32)
    for g in range(GROUP - 2, -1, -1):
        col = jnp.where(r < (g + 1) * rows_per_head, sinks_ref[kv * GROUP + g], col)
    return col


def _swa_prompt_kernel(sinks_ref, q_ref, kc_ref, kp_ref, vc_ref, vp_ref, o_ref):
    j = pl.program_id(1)
    blk = q_ref.shape[0]
    rows = GROUP * blk
    tq = lax.broadcasted_iota(I32, (rows, 2 * blk), 0) % blk + blk
    ts = lax.broadcasted_iota(I32, (rows, 2 * blk), 1)
    first_key = jnp.where(j > 0, 0, blk)
    valid = (tq - ts >= 0) & (tq - ts <= WINDOW) & (ts >= first_key)
    for kv in range(N_KV_HEADS):
        ksl = slice(kv * HEAD_DIM, (kv + 1) * HEAD_DIM)
        kk = jnp.concatenate([kp_ref[:, ksl], kc_ref[:, ksl]], axis=0).astype(BF16)
        vv = jnp.concatenate([vp_ref[:, ksl], vc_ref[:, ksl]], axis=0).astype(BF16)
        q = jnp.concatenate(
            [q_ref[:, (kv * GROUP + g) * HEAD_DIM:(kv * GROUP + g + 1) * HEAD_DIM] for g in range(GROUP)],
            axis=0).astype(BF16)
        s = jnp.where(valid, _dot_nt(q, kk) * SWA_SCALE, -jnp.inf)
        sink = _sink_column(sinks_ref, kv, blk)
        mx = jnp.maximum(jnp.max(s, axis=-1, keepdims=True), sink)
        e = jnp.exp(s - mx)
        p = e / (jnp.sum(e, axis=-1, keepdims=True) + jnp.exp(sink - mx))
        o = jnp.dot(p.astype(BF16), vv, preferred_element_type=F32)
        for g in range(GROUP):
            hsl = slice((kv * GROUP + g) * HEAD_DIM, (kv * GROUP + g + 1) * HEAD_DIM)
            o_ref[:, hsl] = o[g * blk:(g + 1) * blk].astype(o_ref.dtype)


def swa_prompt_attention(qkv, sinks, batch, seq):
    blk = WINDOW
    nb = seq // blk
    dq = N_KV_HEADS * GROUP * HEAD_DIM
    dkv = N_KV_HEADS * HEAD_DIM
    kcol, vcol = dq // dkv, dq // dkv + 1
    cur = lambda b, j: b * nb + j
    prev = lambda b, j: b * nb + jnp.maximum(j - 1, 0)
    return pl.pallas_call(
        _swa_prompt_kernel,
        grid=(batch, nb),
        in_specs=[pl.BlockSpec(memory_space=pltpu.SMEM),
                  pl.BlockSpec((blk, dq), lambda b, j: (cur(b, j), 0)),
                  pl.BlockSpec((blk, dkv), lambda b, j: (cur(b, j), kcol)),
                  pl.BlockSpec((blk, dkv), lambda b, j: (prev(b, j), kcol)),
                  pl.BlockSpec((blk, dkv), lambda b, j: (cur(b, j), vcol)),
                  pl.BlockSpec((blk, dkv), lambda b, j: (prev(b, j), vcol))],
        out_specs=pl.BlockSpec((blk, dq), lambda b, j: (cur(b, j), 0)),
        out_shape=jax.ShapeDtypeStruct((batch * seq, dq), BF16),
        compiler_params=_params("parallel", "arbitrary"),
        name="swa_prompt",
    )(sinks, qkv, qkv, qkv, qkv, qkv)


def _swa_sample_kernel(sinks_ref, q_ref, kn_ref, vn_ref, kc_ref, vc_ref, o_ref):
    n_seq, n_new = kn_ref.shape[0], kn_ref.shape[1]
    rows = GROUP * n_new
    t = lax.broadcasted_iota(I32, (rows, WINDOW), 0) % n_new
    jc = lax.broadcasted_iota(I32, (rows, WINDOW), 1)
    valid_c = jc >= t
    t1 = lax.broadcasted_iota(I32, (rows, 1), 0) % n_new

    def one_seq(s, carry):
        for kv in range(N_KV_HEADS):
            ksl = slice(kv * HEAD_DIM, (kv + 1) * HEAD_DIM)
            q = q_ref[s, kv * rows:(kv + 1) * rows, :]
            qb = q.astype(BF16)
            s_c = jnp.where(valid_c, _dot_nt(qb, kc_ref[s, :, ksl].astype(BF16)) * SWA_SCALE, -jnp.inf)
            kn = kn_ref[s, :, ksl]
            vn = vn_ref[s, :, ksl]
            s_n = [jnp.where(t1 >= i, jnp.sum(q * kn[i:i + 1, :], axis=-1, keepdims=True) * SWA_SCALE, -jnp.inf)
                   for i in range(n_new)]
            sink = _sink_column(sinks_ref, kv, n_new)
            mx = jnp.maximum(jnp.max(s_c, axis=-1, keepdims=True), sink)
            for sn in s_n:
                mx = jnp.maximum(mx, sn)
            e_c = jnp.exp(s_c - mx)
            e_n = [jnp.exp(sn - mx) for sn in s_n]
            den = jnp.sum(e_c, axis=-1, keepdims=True) + jnp.exp(sink - mx)
            for en in e_n:
                den = den + en
            o = jnp.dot((e_c / den).astype(BF16), vc_ref[s, :, ksl].astype(BF16), preferred_element_type=F32)
            for i in range(n_new):
                o = o + (e_n[i] / den) * vn[i:i + 1, :]
            o_ref[s, kv * rows:(kv + 1) * rows, :] = o.astype(o_ref.dtype)
        return carry

    lax.fori_loop(0, n_seq, one_seq, 0)


def swa_sample_attention(q, k_new, v_new, cache_k, cache_v, sinks):
    n, t, dkv = k_new.shape
    sb = _pick(n, (8, 4, 2, 1))
    rows = q.shape[1]
    return pl.pallas_call(
        _swa_sample_kernel,
        grid=(n // sb,),
        in_specs=[pl.BlockSpec(memory_space=pltpu.SMEM),
                  pl.BlockSpec((sb, rows, HEAD_DIM), lambda i: (i, 0, 0)),
                  pl.BlockSpec((sb, t, dkv), lambda i: (i, 0, 0)),
                  pl.BlockSpec((sb, t, dkv), lambda i: (i, 0, 0)),
                  pl.BlockSpec((sb, WINDOW, dkv), lambda i: (i, 0, 0)),
                  pl.BlockSpec((sb, WINDOW, dkv), lambda i: (i, 0, 0))],
        out_specs=pl.BlockSpec((sb, rows, HEAD_DIM), lambda i: (i, 0, 0)),
        out_shape=jax.ShapeDtypeStruct((n, rows, HEAD_DIM), BF16),
        compiler_params=_params("parallel"),
        name="swa_sample",
    )(sinks, q, k_new, v_new, cache_k, cache_v)


def _mla_kv_post_kernel(kv_ref, g_ref, cos_ref, sin_ref, c_ref, pe_ref, cb_ref, peb_ref):
    x = kv_ref[...]
    c = _rms(x[:, :KV_LORA], g_ref[...])
    half = QK_ROPE // 2
    x1, x2 = x[:, KV_LORA:KV_LORA + half], x[:, KV_LORA + half:]
    cos, sin = cos_ref[...], sin_ref[...]
    pe = jnp.concatenate([x1 * cos - x2 * sin, x2 * cos + x1 * sin], axis=-1)
    c_ref[...] = c
    pe_ref[...] = pe
    cb_ref[...] = c.astype(BF16)
    peb_ref[...] = pe.astype(BF16)


def mla_kv_post(kv, g_kv, cos, sin):
    t = kv.shape[0]
    bt = _pick(t, (512, 256, 128, 64, 32, 16, 8))
    half = QK_ROPE // 2
    row = lambda w: pl.BlockSpec((bt, w), lambda i: (i, 0))
    return pl.pallas_call(
        _mla_kv_post_kernel,
        grid=(t // bt,),
        in_specs=[row(KV_LORA + QK_ROPE), pl.BlockSpec((1, KV_LORA), lambda i: (0, 0)), row(half), row(half)],
        out_specs=[row(KV_LORA), row(QK_ROPE), row(KV_LORA), row(QK_ROPE)],
        out_shape=[jax.ShapeDtypeStruct((t, KV_LORA), F32), jax.ShapeDtypeStruct((t, QK_ROPE), F32),
                   jax.ShapeDtypeStruct((t, KV_LORA), BF16), jax.ShapeDtypeStruct((t, QK_ROPE), BF16)],
        compiler_params=_params("parallel"),
        name="mla_kv_post",
    )(kv, g_kv.reshape(1, KV_LORA), cos, sin)


def _mla_absorb_kernel(qn_ref, qp_ref, w_ref, ql_ref, qph_ref):
    h = pl.program_id(0)
    ql_ref[0] = _dot_nt(qn_ref[...], w_ref[...].astype(BF16)).astype(ql_ref.dtype)
    pair = qp_ref[...]
    qph_ref[0] = jnp.where(h % 2 == 0, pair[:, :QK_ROPE], pair[:, QK_ROPE:])


def mla_absorb(q_nope, q_pe, w_uk):
    t = q_nope.shape[0]
    bm = _pick(t, (1088, 1024, 512, 256, 128, 64, 32, 16, 8))
    return pl.pallas_call(
        _mla_absorb_kernel,
        grid=(MLA_HEADS, t // bm),
        in_specs=[pl.BlockSpec((bm, QK_NOPE), lambda h, i: (i, h)),
                  pl.BlockSpec((bm, 2 * QK_ROPE), lambda h, i: (i, h // 2)),
                  pl.BlockSpec((KV_LORA, QK_NOPE), lambda h, i: (0, h))],
        out_specs=[pl.BlockSpec((1, bm, KV_LORA), lambda h, i: (h, i, 0)),
                   pl.BlockSpec((1, bm, QK_ROPE), lambda h, i: (h, i, 0))],
        out_shape=[jax.ShapeDtypeStruct((MLA_HEADS, t, KV_LORA), BF16),
                   jax.ShapeDtypeStruct((MLA_HEADS, t, QK_ROPE), BF16)],
        compiler_params=_params("parallel", "arbitrary"),
        name="mla_absorb",
    )(q_nope, q_pe, w_uk)


def _mla_uv_kernel(o_ref, w_ref, out_ref):
    out_ref[...] = jnp.dot(o_ref[0], w_ref[...].astype(BF16), preferred_element_type=F32).astype(out_ref.dtype)


def mla_uv(o_lat, w_uv):
    t = o_lat.shape[1]
    bm = _pick(t, (1024, 512, 256, 128, 64, 32, 16, 8))
    return pl.pallas_call(
        _mla_uv_kernel,
        grid=(MLA_HEADS, t // bm),
        in_specs=[pl.BlockSpec((1, bm, KV_LORA), lambda h, i: (h, i, 0)),
                  pl.BlockSpec((KV_LORA, V_HEAD), lambda h, i: (0, h))],
        out_specs=pl.BlockSpec((bm, V_HEAD), lambda h, i: (i, h)),
        out_shape=jax.ShapeDtypeStruct((t, MLA_HEADS * V_HEAD), BF16),
        compiler_params=_params("parallel", "arbitrary"),
        name="mla_uv",
    )(o_lat, w_uv)


def _mla_prompt_kernel(qn_ref, qp_ref, c_ref, pe_ref, wuk_ref, wuv_ref, o_ref, kn_sc, v_sc, *, tq):
    h = pl.program_id(1)
    seq = c_ref.shape[0]
    c = c_ref[...]
    kn_sc[...] = jnp.dot(c, wuk_ref[...].astype(BF16), preferred_element_type=F32).astype(BF16)
    v_sc[...] = jnp.dot(c, wuv_ref[...].astype(BF16), preferred_element_type=F32).astype(BF16)
    for qb in range(seq // tq):
        rows = slice(qb * tq, (qb + 1) * tq)
        n_keys = (qb + 1) * tq
        pair = qp_ref[rows, :]
        qp = jnp.where(h % 2 == 0, pair[:, :QK_ROPE], pair[:, QK_ROPE:])
        s = (_dot_nt(qn_ref[rows, :], kn_sc[:n_keys, :]) + _dot_nt(qp, pe_ref[:n_keys, :])) * MLA_SCALE
        row = lax.broadcasted_iota(I32, s.shape, 0) + qb * tq
        col = lax.broadcasted_iota(I32, s.shape, 1)
        s = jnp.where(col <= row, s, -jnp.inf)
        p = jnp.exp(s - jnp.max(s, axis=-1, keepdims=True))
        o = jnp.dot(p.astype(BF16), v_sc[:n_keys, :], preferred_element_type=F32)
        o_ref[rows, :] = (o / jnp.sum(p, axis=-1, keepdims=True)).astype(o_ref.dtype)


def mla_prompt_attention(q_nope, q_pe, c, k_pe, w_uk, w_uv, batch, seq):
    tq = _pick(seq, (256, 128, 64, 32, 16))
    head_cols = lambda w: pl.BlockSpec((seq, w), lambda b, h: (b, h))
    head_w = lambda w: pl.BlockSpec((KV_LORA, w), lambda b, h: (0, h))
    return pl.pallas_call(
        functools.partial(_mla_prompt_kernel, tq=tq),
        grid=(batch, MLA_HEADS),
        in_specs=[head_cols(QK_NOPE),
                  pl.BlockSpec((seq, 2 * QK_ROPE), lambda b, h: (b, h // 2)),
                  pl.BlockSpec((seq, KV_LORA), lambda b, h: (b, 0)),
                  pl.BlockSpec((seq, QK_ROPE), lambda b, h: (b, 0)),
                  head_w(QK_NOPE), head_w(V_HEAD)],
        out_specs=head_cols(V_HEAD),
        out_shape=jax.ShapeDtypeStruct((batch * seq, MLA_HEADS * V_HEAD), BF16),
        scratch_shapes=[pltpu.VMEM((seq, QK_NOPE), BF16), pltpu.VMEM((seq, V_HEAD), BF16)],
        compiler_params=_params("parallel", "arbitrary"),
        name="mla_prompt",
    )(q_nope, q_pe, c, k_pe, w_uk, w_uv)


def _mla_sample_kernel(pt_ref, ql_ref, qp_ref, *refs, n_pages_step):
    ckv_refs = refs[:n_pages_step]
    kpet_refs = refs[n_pages_step:2 * n_pages_step]
    cn_ref, pn_ref, o_ref, m_sc, l_sc, acc_sc, s_a, s_b, key_a, key_b = refs[2 * n_pages_step:]
    g = pl.program_id(1)
    n_new = cn_ref.shape[1]
    page = ckv_refs[0].shape[1]

    @pl.when(g == 0)
    def _():
        m_sc[...] = jnp.full_like(m_sc, RUNNING_MAX_INIT)
        l_sc[...] = jnp.zeros_like(l_sc)
        acc_sc[...] = jnp.zeros_like(acc_sc)
        s_b[...] = jnp.full_like(s_b, -jnp.inf)
        key_b[...] = jnp.zeros_like(key_b)

    q, qp = ql_ref[0], qp_ref[0]

    def step(s_new, key_new, s_old, key_old):
        m_old = m_sc[...]
        m_fold = jnp.maximum(m_old, jnp.max(s_old[...], axis=-1, keepdims=True))
        alpha = jnp.exp(m_old - m_fold)
        l = alpha * l_sc[...]
        acc = alpha * acc_sc[...]
        for i in range(n_pages_step):
            sl = slice(i * page, (i + 1) * page)
            kc = ckv_refs[i][0].astype(BF16)
            key_new[sl, :] = kc
            s_new[:, sl] = (_dot_nt(q, kc) + jnp.dot(qp, kpet_refs[i][0].astype(BF16),
                                                     preferred_element_type=F32)) * MLA_SCALE
            p = jnp.exp(s_old[:, sl] - m_fold)
            l = l + jnp.sum(p, axis=-1, keepdims=True)
            acc = acc + jnp.dot(p.astype(BF16), key_old[sl, :], preferred_element_type=F32)
        m_sc[...] = m_fold
        l_sc[...] = l
        acc_sc[...] = acc

    @pl.when(g % 2 == 0)
    def _():
        step(s_a, key_a, s_b, key_b)

    @pl.when(g % 2 == 1)
    def _():
        step(s_b, key_b, s_a, key_a)

    @pl.when(g == pl.num_programs(1) - 1)
    def _():
        qf, qpf = q.astype(F32), qp.astype(F32)
        cn, pn = cn_ref[0], pn_ref[0]
        t = lax.broadcasted_iota(I32, (q.shape[0], 1), 0) % n_new
        s_n = [jnp.where(t >= i,
                         (jnp.sum(qf * cn[i:i + 1, :], axis=-1, keepdims=True)
                          + jnp.sum(qpf * pn[i:i + 1, :], axis=-1, keepdims=True)) * MLA_SCALE,
                         -jnp.inf) for i in range(n_new)]
        m_old = m_sc[...]
        m_fin = m_old
        for s in s_n:
            m_fin = jnp.maximum(m_fin, s)
        alpha = jnp.exp(m_old - m_fin)
        l = alpha * l_sc[...]
        acc = alpha * acc_sc[...]
        for i, s in enumerate(s_n):
            p = jnp.exp(s - m_fin)
            l = l + p
            acc = acc + p * cn[i:i + 1, :]
        o_ref[0] = (acc / l).astype(o_ref.dtype)


def mla_sample_attention(q_lat, q_pe, cache_ckv, cache_kpe_t, page_table, c_new, pe_new):
    n, rows, _ = q_lat.shape
    n_pages = page_table.shape[1]
    page = cache_ckv.shape[1]
    t_new = c_new.shape[1]
    gp = _pick(n_pages, (PAGES_PER_STEP, 4, 2, 1))
    n_groups = n_pages // gp

    def page_spec(shape, i):
        return pl.BlockSpec((1,) + shape,
                            lambda s, g, pt: (pt[s * n_pages + jnp.minimum(g, n_groups - 1) * gp + i], 0, 0))

    seq_spec = lambda r, w: pl.BlockSpec((1, r, w), lambda s, g, pt: (s, 0, 0))
    scores = pltpu.VMEM((rows, gp * page), F32)
    keys = pltpu.VMEM((gp * page, KV_LORA), BF16)
    return pl.pallas_call(
        functools.partial(_mla_sample_kernel, n_pages_step=gp),
        grid_spec=pltpu.PrefetchScalarGridSpec(
            num_scalar_prefetch=1,
            grid=(n, n_groups + 1),
            in_specs=([seq_spec(rows, KV_LORA), seq_spec(rows, QK_ROPE)]
                      + [page_spec((page, KV_LORA), i) for i in range(gp)]
                      + [page_spec((QK_ROPE, page), i) for i in range(gp)]
                      + [seq_spec(t_new, KV_LORA), seq_spec(t_new, QK_ROPE)]),
            out_specs=seq_spec(rows, KV_LORA),
            scratch_shapes=[pltpu.VMEM((rows, 1), F32), pltpu.VMEM((rows, 1), F32), pltpu.VMEM((rows, KV_LORA), F32),
                            scores, scores, keys, keys],
        ),
        out_shape=jax.ShapeDtypeStruct((n, rows, KV_LORA), BF16),
        compiler_params=_params("parallel", "arbitrary"),
        name="mla_sample",
    )(page_table.reshape(-1), q_lat, q_pe, *([cache_ckv] * gp), *([cache_kpe_t] * gp), c_new, pe_new)


def _router_kernel(x_ref, y_ref, g_ref, wt_ref, b_ref, xo_ref, h_ref, idx_ref, gate_ref):
    x = x_ref[...] + y_ref[...]
    xo_ref[...] = x
    h = _rms(x, g_ref[...])
    h_ref[...] = h
    logits = lax.dot_general(wt_ref[...], h, (((1,), (1,)), ((), ())), precision=lax.Precision.HIGHEST,
                             preferred_element_type=F32) + b_ref[...]
    e = lax.broadcasted_iota(I32, logits.shape, 0)
    v1 = jnp.max(logits, axis=0, keepdims=True)
    i1 = jnp.min(jnp.where(logits == v1, e, N_EXPERTS), axis=0, keepdims=True)
    rest = jnp.where(e == i1, -jnp.inf, logits)
    v2 = jnp.max(rest, axis=0, keepdims=True)
    i2 = jnp.min(jnp.where(rest == v2, e, N_EXPERTS), axis=0, keepdims=True)
    ex = jnp.exp(v2 - v1)
    idx_ref[...] = jnp.concatenate([i1, i2], axis=0)
    gate_ref[...] = jnp.concatenate([1.0 / (1.0 + ex), ex / (1.0 + ex)], axis=0)


def router(x, y, g, w_router, b_router):
    t, d = x.shape
    bt = _pick(t, (128,))
    row = pl.BlockSpec((bt, d), lambda i: (i, 0))
    top = pl.BlockSpec((TOP_K, bt), lambda i: (0, i))
    return pl.pallas_call(
        _router_kernel,
        grid=(t // bt,),
        in_specs=[row, row, pl.BlockSpec((1, d), lambda i: (0, 0)),
                  pl.BlockSpec((N_EXPERTS, d), lambda i: (0, 0)), pl.BlockSpec((N_EXPERTS, 1), lambda i: (0, 0))],
        out_specs=[row, row, top, top],
        out_shape=[jax.ShapeDtypeStruct((t, d), F32), jax.ShapeDtypeStruct((t, d), F32),
                   jax.ShapeDtypeStruct((TOP_K, t), I32), jax.ShapeDtypeStruct((TOP_K, t), F32)],
        compiler_params=_params("parallel"),
        name="router",
    )(x, y, g.reshape(1, d), w_router.T, b_router.reshape(N_EXPERTS, 1))


def _row_copy(src_ref, dst_ref, src_row, dst_row, sem):
    return pltpu.make_async_copy(src_ref.at[pl.ds(src_row, 1), :], dst_ref.at[pl.ds(dst_row, 1), :], sem)


def _gather_kernel(idx_ref, act_ref, src_ref, o_ref, buf, sem):
    rows = buf.shape[0]
    step = pl.program_id(0)
    base = step * rows

    def start(r, c):
        _row_copy(src_ref, buf, idx_ref[base + r], r, sem).start()
        return c

    def wait(r, c):
        _row_copy(src_ref, buf, 0, r, sem).wait()
        return c

    @pl.when(act_ref[step] != 0)
    def _():
        lax.fori_loop(0, rows, start, 0, unroll=DMA_LOOP_UNROLL)
        lax.fori_loop(0, rows, wait, 0, unroll=DMA_LOOP_UNROLL)
        o_ref[...] = buf[...].astype(o_ref.dtype)

    @pl.when(act_ref[step] == 0)
    def _():
        o_ref[...] = jnp.zeros_like(o_ref)


def gather_rows_bf16(src, idx, step_active):
    p = idx.shape[0]
    d = src.shape[1]
    rows = p // step_active.shape[0]
    return pl.pallas_call(
        _gather_kernel,
        grid_spec=pltpu.PrefetchScalarGridSpec(
            num_scalar_prefetch=2,
            grid=(p // rows,),
            in_specs=[pl.BlockSpec(memory_space=pl.ANY)],
            out_specs=pl.BlockSpec((rows, d), lambda i, idx, n: (i, 0)),
            scratch_shapes=[pltpu.VMEM((rows, d), F32), pltpu.SemaphoreType.DMA(())],
        ),
        out_shape=jax.ShapeDtypeStruct((p, d), BF16),
        compiler_params=_params("arbitrary"),
        name="moe_dispatch",
    )(idx, step_active, src)


def _combine_kernel(pos_ref, x_ref, gate_ref, g_ref, y_ref, oa_ref, ob_ref, buf, sem, *, n_a):
    rows = x_ref.shape[0]
    step = pl.program_id(0)
    base = step * rows

    def start(r, c):
        for k in range(TOP_K):
            _row_copy(y_ref, buf, pos_ref[TOP_K * (base + r) + k], k * rows + r, sem).start()
        return c

    def wait(r, c):
        _row_copy(y_ref, buf, 0, r, sem).wait()
        return c

    lax.fori_loop(0, rows, start, 0, unroll=DMA_LOOP_UNROLL)
    lax.fori_loop(0, TOP_K * rows, wait, 0, unroll=DMA_LOOP_UNROLL)
    gates = gate_ref[...]
    x = x_ref[...]
    for k in range(TOP_K):
        x = x + gates[:, k:k + 1] * buf[k * rows:(k + 1) * rows, :]
    out = _rms(x, g_ref[...])

    @pl.when(step < n_a)
    def _():
        oa_ref[...] = out

    @pl.when(step >= n_a)
    def _():
        ob_ref[...] = out


def moe_combine_norm(x, y, pos, gates, g_final, t_a):
    t, d = x.shape
    assert 0 < t_a < t
    rows = _pick(math.gcd(t_a, t - t_a), (COMBINE_ROWS, 64, 32, 16, 8))
    n_a = t_a // rows
    return pl.pallas_call(
        functools.partial(_combine_kernel, n_a=n_a),
        grid_spec=pltpu.PrefetchScalarGridSpec(
            num_scalar_prefetch=1,
            grid=(t // rows,),
            in_specs=[pl.BlockSpec((rows, d), lambda i, pos: (i, 0)),
                      pl.BlockSpec((rows, TOP_K), lambda i, pos: (i, 0)),
                      pl.BlockSpec((1, d), lambda i, pos: (0, 0)),
                      pl.BlockSpec(memory_space=pl.ANY)],
            out_specs=[pl.BlockSpec((rows, d), lambda i, pos: (jnp.minimum(i, n_a - 1), 0)),
                       pl.BlockSpec((rows, d), lambda i, pos: (jnp.maximum(i - n_a, 0), 0))],
            scratch_shapes=[pltpu.VMEM((TOP_K * rows, d), F32), pltpu.SemaphoreType.DMA(())],
        ),
        out_shape=[jax.ShapeDtypeStruct((t_a, d), F32), jax.ShapeDtypeStruct((t - t_a, d), F32)],
        compiler_params=_params("arbitrary"),
        name="moe_combine",
    )(pos, x, gates, g_final.reshape(1, d), y)


def _routing_plan(top_idx, pad, tiles):
    t = top_idx.shape[1]
    n_pairs = t * TOP_K
    n_rows = (n_pairs // pad + N_EXPERTS) * pad
    pair_expert = top_idx.T.reshape(-1)
    order = jnp.argsort(pair_expert, stable=True).astype(I32)
    sorted_expert = pair_expert[order]
    counts = jnp.sum((pair_expert[None, :] == jnp.arange(N_EXPERTS, dtype=I32)[:, None]).astype(I32), axis=1)
    padded = (counts + pad - 1) // pad * pad
    group_end = jnp.cumsum(counts)
    padded_end = jnp.cumsum(padded)
    padded_start = padded_end - padded
    rank = jnp.arange(n_pairs, dtype=I32) - (group_end - counts)[sorted_expert]
    dest = (padded_start[sorted_expert] + rank).astype(I32)
    row_token = jnp.zeros((n_rows,), I32).at[dest].set(order // TOP_K)
    pair_row = jnp.zeros((n_pairs,), I32).at[order].set(dest)
    real_end = padded_start + counts
    last_expert = jnp.sum((padded_end[-1] - 1 >= padded_end).astype(I32))
    tile_maps = []
    for tm in tiles:
        start = jnp.arange(n_rows // tm, dtype=I32) * tm
        expert = jnp.sum((start[:, None] >= padded_end[None, :]).astype(I32), axis=1)
        in_use = expert < N_EXPERTS
        expert = jnp.where(in_use, expert, last_expert)
        active = in_use & (start < real_end[expert])
        n_tiles = n_rows // tm
        tile_ids = jnp.arange(n_tiles, dtype=I32)
        next_active = lax.cummin(jnp.where(active, tile_ids, n_tiles), axis=0, reverse=True)
        expert = jnp.where(next_active < n_tiles, expert[jnp.minimum(next_active, n_tiles - 1)], expert)
        tile_maps.append((expert.astype(I32), active.astype(I32)))
    return row_token, pair_row, tile_maps


def kernel(x_prompt, x_sample, cache_swa_k, cache_swa_v, cache_mla_ckv, cache_mla_kpe, page_table,
           g_attn0, w_qkv_swa, sinks, w_o_swa, g_ffn0, w_ffn_gate, w_ffn_up, w_ffn_down,
           g_attn1, w_dq, g_q, w_uq, w_dkv, g_kv, w_uk, w_uv, w_o_mla,
           g_ffn1, w_router, b_router, w_exp_gate, w_exp_up, w_exp_down, g_final):
    batch, seq, d = x_prompt.shape
    n_seq, t_new, _ = x_sample.shape
    tp, ts = batch * seq, n_seq * t_new
    past_len = page_table.shape[1] * cache_mla_ckv.shape[1]
    n_heads = N_KV_HEADS * GROUP
    dq, dkv = n_heads * HEAD_DIM, N_KV_HEADS * HEAD_DIM

    x0 = jnp.concatenate([x_prompt.reshape(tp, d), x_sample.reshape(ts, d)], axis=0)
    pos = jnp.concatenate([jnp.tile(jnp.arange(seq), batch), jnp.tile(past_len + jnp.arange(t_new), n_seq)])
    _, _, cos_swa, sin_swa = _rope_tables(pos, HEAD_DIM)
    cos_pe, sin_pe, cos_mla, sin_mla = _rope_tables(pos, QK_ROPE)

    h = rmsnorm(x0, g_attn0, BF16)
    qkv = matmul(h, w_qkv_swa, out_dtype=F32, rope=(cos_swa, sin_swa, HEAD_DIM // 2), n_rope_cols=dq + dkv)
    attn_p = swa_prompt_attention(qkv, sinks, batch, seq)
    qkv_s = qkv[tp:]
    q_s = (qkv_s[:, :dq].reshape(n_seq, t_new, N_KV_HEADS, GROUP, HEAD_DIM)
           .transpose(0, 2, 3, 1, 4).reshape(n_seq, N_KV_HEADS * GROUP * t_new, HEAD_DIM))
    k_s = qkv_s[:, dq:dq + dkv].reshape(n_seq, t_new, dkv)
    v_s = qkv_s[:, dq + dkv:].reshape(n_seq, t_new, dkv)
    attn_s = swa_sample_attention(q_s, k_s, v_s, cache_swa_k.reshape(n_seq, WINDOW, dkv),
                                  cache_swa_v.reshape(n_seq, WINDOW, dkv), sinks)
    attn_s = (attn_s.reshape(n_seq, N_KV_HEADS, GROUP, t_new, HEAD_DIM)
              .transpose(0, 3, 1, 2, 4).reshape(ts, dq))
    y = matmul(jnp.concatenate([attn_p, attn_s], axis=0), w_o_swa, out_dtype=F32)

    def window_tail(col):
        tails = [qkv[(b + 1) * seq - WINDOW:(b + 1) * seq, col:col + dkv] for b in range(batch)]
        return jnp.stack(tails).reshape(batch, WINDOW, N_KV_HEADS, HEAD_DIM)

    swa_k_prompt, swa_v_prompt = window_tail(dq), window_tail(dq + dkv)
    swa_k_sample = jnp.concatenate([cache_swa_k[:, t_new:], k_s.reshape(n_seq, t_new, N_KV_HEADS, HEAD_DIM)], axis=1)
    swa_v_sample = jnp.concatenate([cache_swa_v[:, t_new:], v_s.reshape(n_seq, t_new, N_KV_HEADS, HEAD_DIM)], axis=1)

    x1, h = add_rmsnorm(x0, y, g_ffn0)
    y = ffn_dense_down(ffn_dense_hidden(h, w_ffn_gate, w_ffn_up), w_ffn_down)

    x2, h = add_rmsnorm(x1, y, g_attn1)
    cq = rmsnorm(matmul(h, w_dq, out_dtype=F32), g_q, BF16)
    c, k_pe, c_b, k_pe_b = mla_kv_post(matmul(h, w_dkv, out_dtype=F32), g_kv, cos_pe, sin_pe)
    w_uq_h = w_uq.reshape(w_uq.shape[0], MLA_HEADS, QK_NOPE + QK_ROPE)
    q_nope = matmul(cq, w_uq_h[:, :, :QK_NOPE].reshape(-1, MLA_HEADS * QK_NOPE), out_dtype=BF16)
    q_pe = matmul(cq, w_uq_h[:, :, QK_NOPE:].reshape(-1, MLA_HEADS * QK_ROPE), out_dtype=BF16,
                  rope=(cos_mla, sin_mla, QK_ROPE // 2), n_rope_cols=MLA_HEADS * QK_ROPE)
    w_uk2 = w_uk.reshape(KV_LORA, MLA_HEADS * QK_NOPE)
    w_uv2 = w_uv.reshape(KV_LORA, MLA_HEADS * V_HEAD)
    o_p = mla_prompt_attention(q_nope, q_pe, c_b, k_pe_b, w_uk2, w_uv2, batch, seq)

    q_lat, q_pe_h = mla_absorb(q_nope[tp:], q_pe[tp:], w_uk2)

    def per_seq(a):
        return a.reshape(MLA_HEADS, n_seq, t_new, -1).transpose(1, 0, 2, 3).reshape(n_seq, MLA_HEADS * t_new, -1)

    c_s = c[tp:].reshape(n_seq, t_new, KV_LORA)
    pe_s = k_pe[tp:].reshape(n_seq, t_new, QK_ROPE)
    o_lat_s = mla_sample_attention(per_seq(q_lat), per_seq(q_pe_h), cache_mla_ckv,
                                   cache_mla_kpe.transpose(0, 2, 1), page_table, c_s, pe_s)
    o_lat_s = (o_lat_s.reshape(n_seq, MLA_HEADS, t_new, KV_LORA).transpose(1, 0, 2, 3).reshape(MLA_HEADS, ts, KV_LORA))
    o = jnp.concatenate([o_p, mla_uv(o_lat_s, w_uv2)], axis=0)
    y = matmul(o, w_o_mla, out_dtype=F32)

    x3, hn, top_idx, gates = router(x2, y, g_ffn1, w_router, b_router)
    row_token, pair_row, ((te_h, act_h), (te_d, act_d), (_, act_g)) = _routing_plan(
        top_idx, MOE_DOWN_TILE, (MOE_TILE, MOE_DOWN_TILE, GATHER_ROWS))
    a = gather_rows_bf16(hn, row_token, act_g)
    y = moe_down(moe_hidden(a, w_exp_gate, w_exp_up, te_h, act_h), w_exp_down, te_d, act_d)
    out_p, out_s = moe_combine_norm(x3, y, pair_row, gates.T, g_final, tp)

    return (out_p.reshape(batch, seq, d), out_s.reshape(n_seq, t_new, d),
            swa_k_prompt, swa_v_prompt, swa_k_sample, swa_v_sample,
            c[:tp].reshape(batch, seq, KV_LORA), k_pe[:tp].reshape(batch, seq, QK_ROPE), c_s, pe_s)
```
